```python
import jax, jax.numpy as jnp
from jax import lax
import numpy as np

D_MODEL = 1024
BATCH = 2
SEQ = 8192
DEPTH = 2
DEC_BATCH = 32
DEC_SEQ = 4
PAST_LEN = 16384
PAGE_SIZE = 128

HEAD_DIM = 64
N_HEADS_MOBA = 8
N_HEADS_FOX = 8
D_MOBA = N_HEADS_MOBA * HEAD_DIM
D_FOX = N_HEADS_FOX * HEAD_DIM
MOBA_BLOCK = 256
MOBA_TOPK = 3
MOBA_Q_BLOCK = 64
FOX_Q_BLOCK = 128
ROPE_THETA = 10000.0
FORGET_BIAS = 3.0
N_GROUPS = 4
EXPERTS_PER_GROUP = 4
N_EXPERTS = N_GROUPS * EXPERTS_PER_GROUP
D_EXPERT = 4 * D_MODEL // N_EXPERTS
TOPK_IN_GROUP = 2
LN_EPS = 1e-5
DEEPNORM_ALPHA = (2 * DEPTH) ** 0.25
DEEPNORM_BETA = (8 * DEPTH) ** -0.25
PROJ_SIZES = (D_MOBA, D_MOBA, D_MOBA, D_FOX, D_FOX, D_FOX, N_HEADS_FOX, D_MODEL, D_MODEL)
D_IN_PROJ = sum(PROJ_SIZES)

kernel_name = "moba_fox_gated_hmoe_deepnorm_step"


def layer_norm(x, g, b):
    xf = x.astype(jnp.float32)
    mu = jnp.mean(xf, axis=-1, keepdims=True)
    var = jnp.mean(jnp.square(xf - mu), axis=-1, keepdims=True)
    return ((xf - mu) * lax.rsqrt(var + LN_EPS) * g.astype(jnp.float32) + b.astype(jnp.float32)).astype(x.dtype)


def rope(x, pos):
    half = HEAD_DIM // 2
    inv_freq = jnp.power(ROPE_THETA, -jnp.arange(half, dtype=jnp.float32) / half)
    ang = pos.astype(jnp.float32)[:, None] * inv_freq[None, :]
    cos = jnp.cos(ang)[None, :, None, :]
    sin = jnp.sin(ang)[None, :, None, :]
    xf = x.astype(jnp.float32)
    x1, x2 = xf[..., :half], xf[..., half:]
    return jnp.concatenate([x1 * cos - x2 * sin, x2 * cos + x1 * sin], axis=-1).astype(x.dtype)


def _query_block(t, blk):
    return blk if t % blk == 0 else t


def moba_chunk(q_c, pos_c, k_blocks, v_blocks, k_means, n_sel):
    B, Qc, H, DH = q_c.shape
    nb = k_means.shape[1]
    qf = q_c.astype(jnp.float32)
    own = pos_c // MOBA_BLOCK
    gate = jnp.einsum('bqhd,bnhd->bhqn', qf, k_means)
    past = jnp.arange(nb)[None, :] < own[:, None]
    gate = jnp.where(past[None, None], gate, -jnp.inf)
    _, top = lax.top_k(gate, n_sel)
    own_b = jnp.broadcast_to(own.astype(top.dtype)[None, None, :, None], (B, H, Qc, 1))
    blk = jnp.concatenate([top, own_b], axis=-1)
    b_idx = jnp.arange(B)[:, None, None, None]
    h_idx = jnp.arange(H)[None, :, None, None]
    k_g = k_blocks[b_idx, blk, :, h_idx].astype(jnp.float32)
    v_g = v_blocks[b_idx, blk, :, h_idx].astype(jnp.float32)
    logits = jnp.einsum('bqhd,bhqnkd->bhqnk', qf, k_g) * (DH ** -0.5)
    sel_valid = jnp.concatenate([jnp.arange(n_sel)[None, :] < own[:, None],
                                 jnp.ones((Qc, 1), dtype=bool)], axis=-1)
    key_pos = blk[..., None] * MOBA_BLOCK + jnp.arange(MOBA_BLOCK)
    mask = sel_valid[None, None, :, :, None] & (key_pos <= pos_c[None, None, :, None, None])
    logits = jnp.where(mask, logits, -jnp.inf).reshape(B, H, Qc, -1)
    p = jax.nn.softmax(logits, axis=-1).reshape(B, H, Qc, n_sel + 1, MOBA_BLOCK)
    return jnp.einsum('bhqnk,bhqnkd->bqhd', p, v_g).astype(q_c.dtype)


def moba_attention(q, k, v, q_pos):
    B, T, H, DH = q.shape
    L = k.shape[1]
    nb = -(-L // MOBA_BLOCK)
    pad = nb * MOBA_BLOCK - L
    k_blocks = jnp.pad(k, ((0, 0), (0, pad), (0, 0), (0, 0))).reshape(B, nb, MOBA_BLOCK, H, DH)
    v_blocks = jnp.pad(v, ((0, 0), (0, pad), (0, 0), (0, 0))).reshape(B, nb, MOBA_BLOCK, H, DH)
    k_means = jnp.mean(k_blocks.astype(jnp.float32), axis=2)
    n_sel = min(MOBA_TOPK, nb)
    qb = _query_block(T, MOBA_Q_BLOCK)
    n_chunks = T // qb
    qc = q.reshape(B, n_chunks, qb, H, DH).transpose(1, 0, 2, 3, 4)
    pc = q_pos.reshape(n_chunks, qb)
    out = lax.map(lambda a: moba_chunk(a[0], a[1], k_blocks, v_blocks, k_means, n_sel), (qc, pc))
    return out.transpose(1, 0, 2, 3, 4).reshape(B, T, H, DH)


def fox_chunk(q_c, cq_c, pos_c, kf, vf, ck):
    s = jnp.einsum('bqhd,bkhd->bhqk', q_c.astype(jnp.float32), kf) * (HEAD_DIM ** -0.5)
    s = s + cq_c[..., None] - ck[:, :, None, :]
    causal = jnp.arange(kf.shape[1])[None, :] <= pos_c[:, None]
    s = jnp.where(causal[None, None], s, -jnp.inf)
    p = jax.nn.softmax(s, axis=-1)
    return jnp.einsum('bhqk,bkhd->bqhd', p, vf).astype(q_c.dtype)


def fox_attention(q, k, v, logf_all, q_pos):
    B, T, H, DH = q.shape
    L = k.shape[1]
    c = jnp.cumsum(logf_all.astype(jnp.float32), axis=1).transpose(0, 2, 1)
    c_q = c[:, :, L - T:]
    kf, vf = k.astype(jnp.float32), v.astype(jnp.float32)
    qb = _query_block(T, FOX_Q_BLOCK)
    n_chunks = T // qb
    qc = q.reshape(B, n_chunks, qb, H, DH).transpose(1, 0, 2, 3, 4)
    cqc = c_q.reshape(B, H, n_chunks, qb).transpose(2, 0, 1, 3)
    pc = q_pos.reshape(n_chunks, qb)
    out = lax.map(lambda a: fox_chunk(a[0], a[1], a[2], kf, vf, c), (qc, cqc, pc))
    return out.transpose(1, 0, 2, 3, 4).reshape(B, T, H, DH)


def token_mixers(x, pos, past, w_in, b_f, w_br_a, w_br_b, w_o):
    B, T, _ = x.shape
    proj = jnp.einsum('btd,de->bte', x, w_in)
    offs = np.cumsum(PROJ_SIZES)[:-1].tolist()
    q_a, k_a, v_a, q_b, k_b, v_b, f_lin, g_a, g_b = jnp.split(proj, offs, axis=-1)
    q_a = rope(q_a.reshape(B, T, N_HEADS_MOBA, HEAD_DIM), pos)
    k_a = rope(k_a.reshape(B, T, N_HEADS_MOBA, HEAD_DIM), pos)
    v_a = v_a.reshape(B, T, N_HEADS_MOBA, HEAD_DIM)
    q_b = q_b.reshape(B, T, N_HEADS_FOX, HEAD_DIM)
    k_b = k_b.reshape(B, T, N_HEADS_FOX, HEAD_DIM)
    v_b = v_b.reshape(B, T, N_HEADS_FOX, HEAD_DIM)
    logf = jax.nn.log_sigmoid(f_lin.astype(jnp.float32) + b_f.astype(jnp.float32))
    if past is None:
        ka, va, kb, vb, lf = k_a, v_a, k_b, v_b, logf
    else:
        pk_a, pv_a, pk_b, pv_b, plf = past
        ka = jnp.concatenate([pk_a, k_a], axis=1)
        va = jnp.concatenate([pv_a, v_a], axis=1)
        kb = jnp.concatenate([pk_b, k_b], axis=1)
        vb = jnp.concatenate([pv_b, v_b], axis=1)
        lf = jnp.concatenate([plf.astype(jnp.float32), logf], axis=1)
    y_a = moba_attention(q_a, ka, va, pos).reshape(B, T, D_MOBA)
    y_b = fox_attention(q_b, kb, vb, lf, pos).reshape(B, T, D_FOX)
    merged = jax.nn.sigmoid(g_a) * (y_a @ w_br_a) + jax.nn.sigmoid(g_b) * (y_b @ w_br_b)
    return merged @ w_o, (k_a, v_a, k_b, v_b, logf)


def hier_moe(x, w_rg, b_rg, w_re, b_re, w_gate, w_up, w_down):
    B, T, _ = x.shape
    g_logits = jnp.einsum('btd,dg->btg', x, w_rg).astype(jnp.float32) + b_rg.astype(jnp.float32)
    g_prob = jax.nn.softmax(g_logits, axis=-1)
    g_sel = jnp.argmax(g_logits, axis=-1)
    p_g = jnp.max(g_prob, axis=-1, keepdims=True)
    e_logits = (jnp.einsum('btd,de->bte', x, w_re).astype(jnp.float32) + b_re.astype(jnp.float32))
    e_logits = e_logits.reshape(B, T, N_GROUPS, EXPERTS_PER_GROUP)
    e_in = jnp.take_along_axis(e_logits, g_sel[..., None, None], axis=2)[:, :, 0]
    e_prob = jax.nn.softmax(e_in, axis=-1)
    top_w, top_i = lax.top_k(e_prob, TOPK_IN_GROUP)
    top_w = top_w / jnp.sum(top_w, axis=-1, keepdims=True)
    expert_id = g_sel[..., None] * EXPERTS_PER_GROUP + top_i
    comb = jnp.sum(jax.nn.one_hot(expert_id, N_EXPERTS, dtype=jnp.float32) * (p_g * top_w)[..., None], axis=-2)
    comb = comb.astype(x.dtype)
    out = jnp.zeros_like(x)
    for e in range(N_EXPERTS):
        h = jax.nn.silu(x @ w_gate[e]) * (x @ w_up[e])
        out = out + comb[..., e:e + 1] * (h @ w_down[e])
    return out


def trunk_layer(x, pos, past, mix_w, norm_w, moe_w):
    ln1_g, ln1_b, ln2_g, ln2_b = norm_w
    mix, rows = token_mixers(x, pos, past, *mix_w)
    x = layer_norm(DEEPNORM_ALPHA * x + mix, ln1_g, ln1_b)
    x = layer_norm(DEEPNORM_ALPHA * x + hier_moe(x, *moe_w), ln2_g, ln2_b)
    return x, rows


def gather_pages(cache, layer, page_table):
    rows = cache[layer, page_table]
    return rows.reshape((page_table.shape[0], page_table.shape[1] * PAGE_SIZE) + cache.shape[3:])


def setup_inputs(seed: int = 0) -> dict:
    key = jax.random.key(seed)
    ks = jax.random.split(key, 24)
    f32 = jnp.float32
    n_pages = PAST_LEN // PAGE_SIZE
    n_used = DEC_BATCH * n_pages
    n_pool = n_used + n_used // 4

    def nrm(k, shape, scale):
        return scale * jax.random.normal(k, shape, f32)

    kv_a = (DEPTH, n_pool, PAGE_SIZE, N_HEADS_MOBA, HEAD_DIM)
    kv_b = (DEPTH, n_pool, PAGE_SIZE, N_HEADS_FOX, HEAD_DIM)
    return {
        "x_prompt": nrm(ks[0], (BATCH, SEQ, D_MODEL), 1.0),
        "x_sample": nrm(ks[1], (DEC_BATCH, DEC_SEQ, D_MODEL), 1.0),
        "cache_k_moba": nrm(ks[2], kv_a, 1.0),
        "cache_v_moba": nrm(ks[3], kv_a, 1.0),
        "cache_k_fox": nrm(ks[4], kv_b, 1.0),
        "cache_v_fox": nrm(ks[5], kv_b, 1.0),
        "cache_logf_fox": jax.nn.log_sigmoid(FORGET_BIAS + jax.random.normal(ks[6], (DEPTH, n_pool, PAGE_SIZE, N_HEADS_FOX), f32)),
        "page_table": jax.random.permutation(ks[7], n_pool)[:n_used].reshape(DEC_BATCH, n_pages).astype(jnp.int32),
        "w_in": nrm(ks[8], (DEPTH, D_MODEL, D_IN_PROJ), D_MODEL ** -0.5),
        "b_f": FORGET_BIAS + nrm(ks[9], (DEPTH, N_HEADS_FOX), 0.1),
        "w_br_a": nrm(ks[10], (DEPTH, D_MOBA, D_MODEL), D_MOBA ** -0.5),
        "w_br_b": nrm(ks[11], (DEPTH, D_FOX, D_MODEL), D_FOX ** -0.5),
        "w_o": nrm(ks[12], (DEPTH, D_MODEL, D_MODEL), DEEPNORM_BETA * D_MODEL ** -0.5),
        "ln1_g": 1.0 + nrm(ks[13], (DEPTH, D_MODEL), 0.05),
        "ln1_b": nrm(ks[14], (DEPTH, D_MODEL), 0.02),
        "w_rg": nrm(ks[15], (DEPTH, D_MODEL, N_GROUPS), D_MODEL ** -0.5),
        "b_rg": nrm(ks[16], (DEPTH, N_GROUPS), 0.01),
        "w_re": nrm(ks[17], (DEPTH, D_MODEL, N_EXPERTS), D_MODEL ** -0.5),
        "b_re": nrm(ks[18], (DEPTH, N_EXPERTS), 0.01),
        "w_gate": nrm(ks[19], (DEPTH, N_EXPERTS, D_MODEL, D_EXPERT), D_MODEL ** -0.5),
        "w_up": nrm(ks[20], (DEPTH, N_EXPERTS, D_MODEL, D_EXPERT), D_MODEL ** -0.5),
        "w_down": nrm(ks[21], (DEPTH, N_EXPERTS, D_EXPERT, D_MODEL), DEEPNORM_BETA * D_EXPERT ** -0.5),
        "ln2_g": 1.0 + nrm(ks[22], (DEPTH, D_MODEL), 0.05),
        "ln2_b": nrm(ks[23], (DEPTH, D_MODEL), 0.02),
    }


def reference(x_prompt, x_sample, cache_k_moba, cache_v_moba, cache_k_fox, cache_v_fox, cache_logf_fox,
              page_table, w_in, b_f, w_br_a, w_br_b, w_o, ln1_g, ln1_b, w_rg, b_rg, w_re, b_re,
              w_gate, w_up, w_down, ln2_g, ln2_b):
    past_len = page_table.shape[1] * PAGE_SIZE
    pos_p = jnp.arange(x_prompt.shape[1], dtype=jnp.int32)
    pos_s = past_len + jnp.arange(x_sample.shape[1], dtype=jnp.int32)
    hp, hs = x_prompt, x_sample
    rows_p, rows_s = [], []
    for l in range(DEPTH):
        mix_w = (w_in[l], b_f[l], w_br_a[l], w_br_b[l], w_o[l])
        norm_w = (ln1_g[l], ln1_b[l], ln2_g[l], ln2_b[l])
        moe_w = (w_rg[l], b_rg[l], w_re[l], b_re[l], w_gate[l], w_up[l], w_down[l])
        hp, rp = trunk_layer(hp, pos_p, None, mix_w, norm_w, moe_w)
        past = (gather_pages(cache_k_moba, l, page_table), gather_pages(cache_v_moba, l, page_table),
                gather_pages(cache_k_fox, l, page_table), gather_pages(cache_v_fox, l, page_table),
                gather_pages(cache_logf_fox, l, page_table))
        hs, rs = trunk_layer(hs, pos_s, past, mix_w, norm_w, moe_w)
        rows_p.append(rp)
        rows_s.append(rs)

    def stack(rows, i):
        return jnp.stack([r[i] for r in rows], axis=0)

    return (hp, hs,
            stack(rows_p, 0), stack(rows_p, 1), stack(rows_p, 2), stack(rows_p, 3), stack(rows_p, 4),
            stack(rows_s, 0), stack(rows_s, 1), stack(rows_s, 2), stack(rows_s, 3), stack(rows_s, 4))
```

```python
import functools

import jax
import jax.numpy as jnp
from jax import lax
from jax.experimental import pallas as pl
from jax.experimental.pallas import tpu as pltpu

F32 = jnp.float32
BF16 = jnp.bfloat16

HEAD_DIM = 64
N_HEADS = 8
D_ATT = N_HEADS * HEAD_DIM
MOBA_BLOCK = 256
MOBA_TOPK = 3
PAGE_SIZE = 128
ROPE_THETA = 10000.0
N_GROUPS = 4
EXPERTS_PER_GROUP = 4
N_EXPERTS = N_GROUPS * EXPERTS_PER_GROUP
LN_EPS = 1e-5

LANES = 128
HEAD_ROWS = 16
NEW_ROWS = 16
PAGES_PER_STEP = 8
MASKED = -1e30
VMEM_LIMIT = 56 * 1024 * 1024


def _dot(a, b):
    return jnp.dot(a, b, preferred_element_type=F32)


def _dot_nt(a, b):
    return lax.dot_general(a, b, (((1,), (1,)), ((), ())), preferred_element_type=F32)


def _split3(x):
    hi = x.astype(BF16)
    r1 = x - hi.astype(F32)
    mid = r1.astype(BF16)
    lo = (r1 - mid.astype(F32)).astype(BF16)
    return hi, mid, lo


def _layer_norm(z, g, b):
    mu = jnp.mean(z, axis=-1, keepdims=True)
    zc = z - mu
    var = jnp.mean(zc * zc, axis=-1, keepdims=True)
    return zc * lax.rsqrt(var + LN_EPS) * g + b


def _log_sigmoid(z):
    return jnp.minimum(z, 0.0) - jnp.log1p(jnp.exp(-jnp.abs(z)))


def _const_spec(shape):
    nd = len(shape)
    return pl.BlockSpec(shape, lambda *_: (0,) * nd, pipeline_mode=pl.Buffered(1))


def _params(*sem):
    return pltpu.CompilerParams(dimension_semantics=sem, vmem_limit_bytes=VMEM_LIMIT)


def _rope_lanes(t, cos, sin):
    lane = lax.broadcasted_iota(jnp.int32, t.shape, 1)
    first_half = (lane & (HEAD_DIM - 1)) < (HEAD_DIM // 2)
    swapped = jnp.where(first_half,
                        pltpu.roll(t, t.shape[1] - HEAD_DIM // 2, 1),
                        pltpu.roll(t, HEAD_DIM // 2, 1))
    return t * cos + swapped * sin


def _rope_rows(t, cos, sin):
    half = HEAD_DIM // 2
    out = []
    for h in range(N_HEADS):
        x1 = t[h * HEAD_DIM:h * HEAD_DIM + half]
        x2 = t[h * HEAD_DIM + half:(h + 1) * HEAD_DIM]
        out += [x1 * cos - x2 * sin, x2 * cos + x1 * sin]
    return jnp.concatenate(out, axis=0)


def _forget_rows(xb, wft_ref, bf_ref):
    return _log_sigmoid(_dot_nt(wft_ref[...], xb) + bf_ref[...])


def _proj_prompt_kernel(x_ref, wn_ref, wt_ref, wg_ref, wft_ref, bf_ref, cos_ref, sin_ref, cost_ref, sint_ref,
                        tri_ref, qa_ref, vab_ref, qb_ref, vbb_ref, kat_ref, vat_ref, kbt_ref, vbt_ref,
                        katb_ref, kbtb_ref, sga_ref, sgb_ref, lft_ref, ckt_ref, kmean_ref, carry_ref,
                        *, tiles_per_seq):
    i = pl.program_id(0)
    tm = x_ref.shape[0]
    xb = x_ref[...].astype(BF16)
    scale = HEAD_DIM ** -0.5

    pn = _dot(xb, wn_ref[...])
    qa_ref[...] = (_rope_lanes(pn[:, :D_ATT], cos_ref[...], sin_ref[...]) * scale).astype(BF16)
    vab_ref[...] = pn[:, D_ATT:2 * D_ATT].astype(BF16)
    qb_ref[...] = (pn[:, 2 * D_ATT:3 * D_ATT] * scale).astype(BF16)
    vbb_ref[...] = pn[:, 3 * D_ATT:].astype(BF16)

    pt = _dot_nt(wt_ref[...], xb)
    kat = _rope_rows(pt[:D_ATT], cost_ref[...], sint_ref[...])
    kat_ref[...] = kat
    katb_ref[...] = kat.astype(BF16)
    kmean_ref[...] = jnp.mean(kat, axis=1, keepdims=True)
    vat_ref[...] = pt[D_ATT:2 * D_ATT]
    kbt = pt[2 * D_ATT:3 * D_ATT]
    kbt_ref[...] = kbt
    kbtb_ref[...] = kbt.astype(BF16)
    vbt_ref[...] = pt[3 * D_ATT:]

    pg = _dot(xb, wg_ref[...])
    d_model = pg.shape[1] // 2
    sga_ref[...] = jax.nn.sigmoid(pg[:, :d_model])
    sgb_ref[...] = jax.nn.sigmoid(pg[:, d_model:])

    lft = _forget_rows(xb, wft_ref, bf_ref)
    lft_ref[...] = lft

    @pl.when(i % tiles_per_seq == 0)
    def _():
        carry_ref[...] = jnp.zeros_like(carry_ref)

    hi, mid, lo = _split3(lft)
    tri = tri_ref[...]
    ck = _dot(hi, tri) + _dot(mid, tri) + _dot(lo, tri) + carry_ref[...]
    ckt_ref[...] = ck
    carry_ref[...] = ck[:, tm - 1:tm]


def _proj_prompt(x2d, wn, wt, wg, wft, bfp, cos, sin, cost, sint, tri, *, batch, tm):
    n, d = x2d.shape
    t = n // batch
    tps = t // tm
    nt = n // tm
    row = lambda w: pl.BlockSpec((tm, w), lambda i: (i, 0))
    tab = pl.BlockSpec((tm, D_ATT), lambda i: (i % tps, 0))
    tabt = pl.BlockSpec((HEAD_DIM // 2, tm), lambda i: (0, i % tps))
    heads_t = pl.BlockSpec((HEAD_ROWS, tm), lambda i: (0, i))
    trans = pl.BlockSpec((None, D_ATT, tm), lambda i: (i // tps, 0, i % tps))
    out_shape = ([jax.ShapeDtypeStruct((n, D_ATT), BF16)] * 4
                 + [jax.ShapeDtypeStruct((batch, D_ATT, t), F32)] * 4
                 + [jax.ShapeDtypeStruct((batch, D_ATT, t), BF16)] * 2
                 + [jax.ShapeDtypeStruct((n, d), F32)] * 2
                 + [jax.ShapeDtypeStruct((HEAD_ROWS, n), F32)] * 2
                 + [jax.ShapeDtypeStruct((nt, D_ATT, 1), F32)])
    out_specs = ([row(D_ATT)] * 4 + [trans] * 6 + [row(d)] * 2 + [heads_t] * 2
                 + [pl.BlockSpec((None, D_ATT, 1), lambda i: (i, 0, 0))])
    return pl.pallas_call(
        functools.partial(_proj_prompt_kernel, tiles_per_seq=tps),
        grid=(nt,),
        in_specs=[row(d), _const_spec(wn.shape), _const_spec(wt.shape), _const_spec(wg.shape),
                  _const_spec(wft.shape), _const_spec(bfp.shape), tab, tab, tabt, tabt,
                  _const_spec(tri.shape)],
        out_specs=out_specs,
        out_shape=out_shape,
        scratch_shapes=[pltpu.VMEM((HEAD_ROWS, 1), F32)],
        compiler_params=_params("arbitrary"),
        name="proj_prompt",
    )(x2d, wn, wt, wg, wft, bfp, cos, sin, cost, sint, tri)


def _proj_decode_kernel(x_ref, wa_ref, wb_ref, wg_ref, wft_ref, bf_ref, cos_ref, sin_ref,
                        qa_ref, ka_ref, va_ref, qb_ref, kb_ref, vb_ref, sga_ref, sgb_ref, lft_ref):
    xb = x_ref[...].astype(BF16)
    scale = HEAD_DIM ** -0.5
    cos, sin = cos_ref[...], sin_ref[...]
    pa = _dot(xb, wa_ref[...])
    qa_ref[...] = (_rope_lanes(pa[:, :D_ATT], cos, sin) * scale).astype(BF16)
    ka_ref[...] = _rope_lanes(pa[:, D_ATT:2 * D_ATT], cos, sin)
    va_ref[...] = pa[:, 2 * D_ATT:]
    pb = _dot(xb, wb_ref[...])
    qb_ref[...] = (pb[:, :D_ATT] * scale).astype(BF16)
    kb_ref[...] = pb[:, D_ATT:2 * D_ATT]
    vb_ref[...] = pb[:, 2 * D_ATT:]
    pg = _dot(xb, wg_ref[...])
    d_model = pg.shape[1] // 2
    sga_ref[...] = jax.nn.sigmoid(pg[:, :d_model])
    sgb_ref[...] = jax.nn.sigmoid(pg[:, d_model:])
    lft_ref[...] = _forget_rows(xb, wft_ref, bf_ref)


def _proj_decode(x2d, wa, wb, wg, wft, bfp, cos, sin):
    n, d = x2d.shape
    full = lambda a: pl.BlockSpec(a.shape, lambda i: (0,) * a.ndim)
    ins = (x2d, wa, wb, wg, wft, bfp, cos, sin)
    out_shape = ([jax.ShapeDtypeStruct((n, D_ATT), BF16)] + [jax.ShapeDtypeStruct((n, D_ATT), F32)] * 2
                 + [jax.ShapeDtypeStruct((n, D_ATT), BF16)] + [jax.ShapeDtypeStruct((n, D_ATT), F32)] * 2
                 + [jax.ShapeDtypeStruct((n, d), F32)] * 2
                 + [jax.ShapeDtypeStruct((HEAD_ROWS, n), F32)])
    return pl.pallas_call(
        _proj_decode_kernel,
        grid=(1,),
        in_specs=[full(a) for a in ins],
        out_specs=[full(s) for s in out_shape],
        out_shape=out_shape,
        compiler_params=_params("arbitrary"),
        name="proj_decode",
    )(*ins)


def _attn_kernel(q_ref, kt_ref, v_ref, aux_ref, o_ref, *, mode):
    hp = pl.program_id(1)
    i = pl.program_id(2)
    tq = q_ref.shape[0]
    q2 = q_ref[...]
    lane = lax.broadcasted_iota(jnp.int32, (tq, LANES), 1)
    rows = lax.broadcasted_iota(jnp.int32, (tq, tq), 0)
    cols = lax.broadcasted_iota(jnp.int32, (tq, tq), 1)
    causal = rows >= cols
    off_d = pl.multiple_of(i * tq, tq)

    outs = []
    for hh in range(2):
        head_lanes = (lane >= HEAD_DIM) if hh else (lane < HEAD_DIM)
        qh = jnp.where(head_lanes, q2, jnp.zeros_like(q2))

        if mode == "moba":
            nb = aux_ref.shape[1]
            gate = _dot(qh, aux_ref[...].astype(BF16))
            blk = lax.broadcasted_iota(jnp.int32, (tq, nb), 1)
            g = jnp.where(blk < i, gate, -jnp.inf)
            sel = jnp.zeros((tq, nb), F32)
            for _ in range(min(MOBA_TOPK, nb)):
                mx = jnp.max(g, axis=1, keepdims=True)
                first = jnp.min(jnp.where(g == mx, blk, nb), axis=1, keepdims=True)
                pick = (blk == first) & (mx > -jnp.inf)
                sel = jnp.where(pick, 1.0, sel)
                g = jnp.where(pick, -jnp.inf, g)

        def logits(off):
            s = _dot(qh, kt_ref[:, pl.ds(off, tq)])
            if mode == "fox":
                s = s - aux_ref[pl.ds(2 * hp + hh, 1), pl.ds(off, tq)]
            return s

        s = jnp.where(causal, logits(off_d), MASKED)
        m = jnp.max(s, axis=1, keepdims=True)
        p = jnp.exp(s - m)
        l = jnp.sum(p, axis=1, keepdims=True)
        acc = _dot(p.astype(BF16), v_ref[pl.ds(off_d, tq), :])

        def body(j, carry):
            m, l, acc = carry
            off = pl.multiple_of(j * tq, tq)
            s = logits(off)
            if mode == "moba":
                chosen = jnp.max(jnp.where(blk == j, sel, 0.0), axis=1, keepdims=True)
                s = jnp.where(chosen > 0.0, s, MASKED)
            m_new = jnp.maximum(m, jnp.max(s, axis=1, keepdims=True))
            a = jnp.exp(m - m_new)
            p = jnp.exp(s - m_new)
            l = a * l + jnp.sum(p, axis=1, keepdims=True)
            acc = a * acc + _dot(p.astype(BF16), v_ref[pl.ds(off, tq), :])
            return m_new, l, acc

        m, l, acc = lax.fori_loop(0, i, body, (m, l, acc))
        outs.append(acc / l)

    o_ref[...] = jnp.where(lane < HEAD_DIM, outs[0], outs[1]).astype(o_ref.dtype)


def _attn_prompt(q, kt, v, aux, *, batch, mode):
    n = q.shape[0]
    t = n // batch
    tq = MOBA_BLOCK
    q3, v3 = q.reshape(batch, t, D_ATT), v.reshape(batch, t, D_ATT)
    if mode == "moba":
        aux_spec = pl.BlockSpec((None, LANES, aux.shape[2]), lambda b, hp, i: (b, hp, 0))
    else:
        aux_spec = pl.BlockSpec((HEAD_ROWS, t), lambda b, hp, i: (0, b))
    out = pl.pallas_call(
        functools.partial(_attn_kernel, mode=mode),
        grid=(batch, D_ATT // LANES, t // tq),
        in_specs=[pl.BlockSpec((None, tq, LANES), lambda b, hp, i: (b, i, hp)),
                  pl.BlockSpec((None, LANES, t), lambda b, hp, i: (b, hp, 0)),
                  pl.BlockSpec((None, t, LANES), lambda b, hp, i: (b, 0, hp)),
                  aux_spec],
        out_specs=pl.BlockSpec((None, tq, LANES), lambda b, hp, i: (b, i, hp)),
        out_shape=jax.ShapeDtypeStruct((batch, t, D_ATT), BF16),
        compiler_params=_params("arbitrary", "arbitrary", "arbitrary"),
        name="attn_" + mode,
    )(q3, kt, v3, aux)
    return out.reshape(n, D_ATT)


def _head_rows_q(q_ref):
    q = q_ref[...]
    r = lax.broadcasted_iota(jnp.int32, q.shape, 0)
    c = lax.broadcasted_iota(jnp.int32, q.shape, 1)
    own = (c // HEAD_DIM) == (r % N_HEADS)
    return jnp.where(own, q, jnp.zeros_like(q)), own


def _head_diag(res, own, t_new):
    kept = jnp.where(own, res, 0.0)
    return jnp.concatenate(
        [jnp.sum(kept[t * N_HEADS:(t + 1) * N_HEADS], axis=0, keepdims=True) for t in range(t_new)], axis=0)


def _sfox_kernel(pt_ref, q_ref, *refs, n_steps, t_new):
    del pt_ref
    npg = PAGES_PER_STEP
    k_refs, v_refs, lf_refs = refs[:npg], refs[npg:2 * npg], refs[2 * npg:3 * npg]
    knew_ref, vnew_ref, lfnew_ref, tri_ref, o_ref, m_scr, l_scr, acc_scr, run_scr = refs[3 * npg:]
    c = pl.program_id(1)
    qbig, own = _head_rows_q(q_ref)

    @pl.when(c == 0)
    def _():
        m_scr[...] = jnp.full_like(m_scr, MASKED)
        l_scr[...] = jnp.zeros_like(l_scr)
        acc_scr[...] = jnp.zeros_like(acc_scr)
        run_scr[...] = jnp.zeros_like(run_scr)

    kt_all = jnp.concatenate([r[...] for r in k_refs], axis=1).astype(BF16)
    vt_all = jnp.concatenate([r[...] for r in v_refs], axis=1).astype(BF16)
    s = _dot(qbig, kt_all)

    lf = jnp.concatenate([r[...] for r in lf_refs], axis=0)
    hi, mid, lo = _split3(lf)
    tri = tri_ref[...]
    local = _dot(hi, tri) + _dot(mid, tri) + _dot(lo, tri)
    run = run_scr[...]
    pieces = []
    for pg in range(npg):
        cp = local[pg * N_HEADS:(pg + 1) * N_HEADS] + run
        pieces.append(cp)
        run = cp[:, PAGE_SIZE - 1:PAGE_SIZE]
    run_scr[...] = run
    ck = jnp.concatenate(pieces, axis=1)
    s = s - jnp.concatenate([ck] * t_new, axis=0)

    m_old = m_scr[...]
    m_new = jnp.maximum(m_old, jnp.max(s, axis=1, keepdims=True))
    a = jnp.exp(m_old - m_new)
    p = jnp.exp(s - m_new)
    l_new = a * l_scr[...] + jnp.sum(p, axis=1, keepdims=True)
    acc_new = a * acc_scr[...] + _dot_nt(p.astype(BF16), vt_all)
    m_scr[...] = m_new
    l_scr[...] = l_new
    acc_scr[...] = acc_new

    @pl.when(c == n_steps - 1)
    def _():
        sn = _dot_nt(qbig, knew_ref[...].astype(BF16))
        lfn = lfnew_ref[...]
        u8 = lax.broadcasted_iota(jnp.int32, lfn.shape, 1)
        cn = run + jnp.zeros_like(lfn)
        for u in range(t_new):
            cn = cn + jnp.where(u8 >= u, lfn[:, u:u + 1], 0.0)
        sn = sn - jnp.concatenate([cn] * t_new, axis=0)
        u = lax.broadcasted_iota(jnp.int32, sn.shape, 1)
        tok = lax.broadcasted_iota(jnp.int32, sn.shape, 0) // N_HEADS
        sn = jnp.where(u <= tok, sn, MASKED)
        m_fin = jnp.maximum(m_new, jnp.max(sn, axis=1, keepdims=True))
        a2 = jnp.exp(m_new - m_fin)
        pn = jnp.exp(sn - m_fin)
        l_fin = a2 * l_new + jnp.sum(pn, axis=1, keepdims=True)
        acc_fin = a2 * acc_new + _dot(pn.astype(BF16), vnew_ref[...].astype(BF16))
        o_ref[...] = _head_diag(acc_fin / l_fin, own, t_new).astype(o_ref.dtype)


def _smoba_kernel(pt_ref, q_ref, *refs, n_steps, t_new):
    del pt_ref
    npg = PAGES_PER_STEP
    k_refs, v_refs = refs[:npg], refs[npg:2 * npg]
    knew_ref, vnew_ref, o_ref, m_scr, l_scr, g_scr, acc_scr = refs[2 * npg:]
    c = pl.program_id(1)
    qbig, own = _head_rows_q(q_ref)
    nrow = qbig.shape[0]
    blocks_per_step = npg * PAGE_SIZE // MOBA_BLOCK
    n_past = n_steps * blocks_per_step
    blk_lane = lax.broadcasted_iota(jnp.int32, (nrow, LANES), 1)

    @pl.when(c == 0)
    def _():
        m_scr[...] = jnp.zeros_like(m_scr)
        l_scr[...] = jnp.zeros_like(l_scr)
        g_scr[...] = jnp.zeros_like(g_scr)

    kt_all = jnp.concatenate([r[...] for r in k_refs], axis=1).astype(BF16)
    vt_all = jnp.concatenate([r[...] for r in v_refs], axis=1).astype(BF16)
    s = _dot(qbig, kt_all)
    m_all, l_all, g_all = m_scr[...], l_scr[...], g_scr[...]
    for jb in range(blocks_per_step):
        sb = s[:, jb * MOBA_BLOCK:(jb + 1) * MOBA_BLOCK]
        gb = jnp.sum(sb, axis=1, keepdims=True)
        mb = jnp.max(sb, axis=1, keepdims=True)
        p = jnp.exp(sb - mb)
        lb = jnp.sum(p, axis=1, keepdims=True)
        blk = c * blocks_per_step + jb
        acc_scr[blk] = _dot_nt(p.astype(BF16), vt_all[:, jb * MOBA_BLOCK:(jb + 1) * MOBA_BLOCK])
        hit = blk_lane == blk
        m_all = jnp.where(hit, mb, m_all)
        l_all = jnp.where(hit, lb, l_all)
        g_all = jnp.where(hit, gb, g_all)
    m_scr[...] = m_all
    l_scr[...] = l_all
    g_scr[...] = g_all

    @pl.when(c == n_steps - 1)
    def _():
        g = jnp.where(blk_lane < n_past, g_all, -jnp.inf)
        sel = blk_lane < 0
        for _ in range(min(MOBA_TOPK, n_past)):
            mx = jnp.max(g, axis=1, keepdims=True)
            first = jnp.min(jnp.where(g == mx, blk_lane, LANES), axis=1, keepdims=True)
            pick = (blk_lane == first) & (mx > -jnp.inf)
            sel = sel | pick
            g = jnp.where(pick, -jnp.inf, g)

        sn = _dot_nt(qbig, knew_ref[...].astype(BF16))
        u = lax.broadcasted_iota(jnp.int32, sn.shape, 1)
        tok = lax.broadcasted_iota(jnp.int32, sn.shape, 0) // N_HEADS
        sn = jnp.where(u <= tok, sn, MASKED)
        m_own = jnp.max(sn, axis=1, keepdims=True)
        p_own = jnp.exp(sn - m_own)
        l_own = jnp.sum(p_own, axis=1, keepdims=True)
        acc_own = _dot(p_own.astype(BF16), vnew_ref[...].astype(BF16))

        m_fin = jnp.maximum(jnp.max(jnp.where(sel, m_all, MASKED), axis=1, keepdims=True), m_own)
        w = jnp.where(sel, jnp.exp(m_all - m_fin), 0.0)
        w_own = jnp.exp(m_own - m_fin)
        l_fin = jnp.sum(jnp.where(sel, w * l_all, 0.0), axis=1, keepdims=True) + w_own * l_own

        def body(j, out):
            wj = jnp.sum(jnp.where(blk_lane == j, w, 0.0), axis=1, keepdims=True)
            return out + wj * acc_scr[j]

        out = lax.fori_loop(0, n_past, body, w_own * acc_own)
        o_ref[...] = _head_diag(out / l_fin, own, t_new).astype(o_ref.dtype)


def _attn_decode(q, k_new, v_new, lf_new, page_table, cache_kt, cache_vt, cache_lft, tri, *, layer, mode):
    bsz, nrow, _ = q.shape
    t_new = nrow // N_HEADS
    n_pages = page_table.shape[1]
    npg = PAGES_PER_STEP
    n_steps = n_pages // npg
    pt_flat = page_table.reshape(-1)

    def page_spec(pg, rows):
        return pl.BlockSpec((None, None, rows, PAGE_SIZE),
                            lambda b, c, pt: (layer, pt[b * n_pages + c * npg + pg], 0, 0))

    per_seq = lambda rows, width: pl.BlockSpec((None, rows, width), lambda b, c, pt: (b, 0, 0))
    kv_specs = [page_spec(pg, D_ATT) for pg in range(npg)]
    in_specs = [per_seq(nrow, D_ATT)] + kv_specs + kv_specs
    args = [q] + [cache_kt] * npg + [cache_vt] * npg
    if mode == "fox":
        in_specs += [page_spec(pg, N_HEADS) for pg in range(npg)]
        args += [cache_lft] * npg
        in_specs += [per_seq(NEW_ROWS, D_ATT), per_seq(NEW_ROWS, D_ATT), per_seq(N_HEADS, NEW_ROWS),
                     pl.BlockSpec(tri.shape, lambda b, c, pt: (0, 0))]
        args += [k_new, v_new, lf_new, tri]
        scratch = [pltpu.VMEM((nrow, 1), F32), pltpu.VMEM((nrow, 1), F32),
                   pltpu.VMEM((nrow, D_ATT), F32), pltpu.VMEM((N_HEADS, 1), F32)]
        body = _sfox_kernel
    else:
        in_specs += [per_seq(NEW_ROWS, D_ATT), per_seq(NEW_ROWS, D_ATT)]
        args += [k_new, v_new]
        n_past = n_steps * npg * PAGE_SIZE // MOBA_BLOCK
        scratch = [pltpu.VMEM((nrow, LANES), F32)] * 3 + [pltpu.VMEM((n_past, nrow, D_ATT), F32)]
        body = _smoba_kernel
    return pl.pallas_call(
        functools.partial(body, n_steps=n_steps, t_new=t_new),
        grid_spec=pltpu.PrefetchScalarGridSpec(
            num_scalar_prefetch=1, grid=(bsz, n_steps), in_specs=in_specs,
            out_specs=pl.BlockSpec((None, t_new, D_ATT), lambda b, c, pt: (b, 0, 0)),
            scratch_shapes=scratch),
        out_shape=jax.ShapeDtypeStruct((bsz, t_new, D_ATT), BF16),
        compiler_params=_params("arbitrary", "arbitrary"),
        name="decode_" + mode,
    )(pt_flat, *args)


def _merge_kernel(x_ref, ya_ref, yb_ref, sga_ref, sgb_ref, wa_ref, wb_ref, wo_ref, g1_ref, b1_ref,
                  wrh_ref, wrl_ref, br_ref, h_ref, comb_ref, *, alpha):
    ua = _dot(ya_ref[...], wa_ref[...])
    ub = _dot(yb_ref[...], wb_ref[...])
    merged = sga_ref[...] * ua + sgb_ref[...] * ub
    mix = _dot(merged.astype(BF16), wo_ref[...])
    h = _layer_norm(alpha * x_ref[...] + mix, g1_ref[...], b1_ref[...])
    h_ref[...] = h

    h_hi = h.astype(BF16)
    h_lo = (h - h_hi.astype(F32)).astype(BF16)
    wrh = wrh_ref[...]
    logit = _dot(h_hi, wrh) + _dot(h_lo, wrh) + _dot(h_hi, wrl_ref[...]) + br_ref[...]
    lane = lax.broadcasted_iota(jnp.int32, logit.shape, 1)
    is_group = (lane >= N_EXPERTS) & (lane < N_EXPERTS + N_GROUPS)
    g_max = jnp.max(jnp.where(is_group, logit, -jnp.inf), axis=1, keepdims=True)
    g_sel = jnp.min(jnp.where(is_group & (logit == g_max), lane - N_EXPERTS, LANES), axis=1, keepdims=True)
    g_sum = jnp.sum(jnp.where(is_group, jnp.exp(logit - g_max), 0.0), axis=1, keepdims=True)
    p_g = 1.0 / g_sum
    in_group = (lane < N_EXPERTS) & ((lane // EXPERTS_PER_GROUP) == g_sel)
    e_max = jnp.max(jnp.where(in_group, logit, -jnp.inf), axis=1, keepdims=True)
    e_exp = jnp.where(in_group, jnp.exp(logit - e_max), 0.0)
    e_prob = e_exp / jnp.sum(e_exp, axis=1, keepdims=True)
    p1 = jnp.max(jnp.where(in_group, e_prob, -1.0), axis=1, keepdims=True)
    i1 = jnp.min(jnp.where(in_group & (e_prob == p1), lane, LANES), axis=1, keepdims=True)
    rest = in_group & (lane != i1)
    p2 = jnp.max(jnp.where(rest, e_prob, -1.0), axis=1, keepdims=True)
    i2 = jnp.min(jnp.where(rest & (e_prob == p2), lane, LANES), axis=1, keepdims=True)
    den = p1 + p2
    comb_ref[...] = p_g * jnp.where(lane == i1, p1 / den, jnp.where(lane == i2, p2 / den, 0.0))


def _merge(x2d, ya, yb, sga, sgb, wa, wb, wo, g1, b1, wrh, wrl, br, *, tm, alpha):
    n, d = x2d.shape
    row = lambda w: pl.BlockSpec((tm, w), lambda i: (i, 0))
    return pl.pallas_call(
        functools.partial(_merge_kernel, alpha=alpha),
        grid=(n // tm,),
        in_specs=[row(d), row(D_ATT), row(D_ATT), row(d), row(d),
                  _const_spec(wa.shape), _const_spec(wb.shape), _const_spec(wo.shape),
                  _const_spec(g1.shape), _const_spec(b1.shape),
                  _const_spec(wrh.shape), _const_spec(wrl.shape), _const_spec(br.shape)],
        out_specs=[row(d), row(LANES)],
        out_shape=[jax.ShapeDtypeStruct((n, d), F32), jax.ShapeDtypeStruct((n, LANES), F32)],
        compiler_params=_params("arbitrary"),
        name="merge",
    )(x2d, ya, yb, sga, sgb, wa, wb, wo, g1, b1, wrh, wrl, br)


def _moe_kernel(h_ref, comb_ref, wg_ref, wu_ref, wd_ref, g2_ref, b2_ref, o_ref, *, alpha):
    h = h_ref[...]
    hb = h.astype(BF16)
    comb = comb_ref[...]
    acc = jnp.zeros_like(h)
    for e in range(N_EXPERTS):
        g = _dot(hb, wg_ref[e])
        u = _dot(hb, wu_ref[e])
        act = g * jax.nn.sigmoid(g) * u * comb[:, e:e + 1]
        acc = acc + _dot(act.astype(BF16), wd_ref[e])
    o_ref[...] = _layer_norm(alpha * h + acc, g2_ref[...], b2_ref[...])


def _moe(h2d, comb, wg, wu, wd, g2, b2, *, tm, alpha):
    n, d = h2d.shape
    row = lambda w: pl.BlockSpec((tm, w), lambda i: (i, 0))
    return pl.pallas_call(
        functools.partial(_moe_kernel, alpha=alpha),
        grid=(n // tm,),
        in_specs=[row(d), row(LANES), _const_spec(wg.shape), _const_spec(wu.shape), _const_spec(wd.shape),
                  _const_spec(g2.shape), _const_spec(b2.shape)],
        out_specs=row(d),
        out_shape=jax.ShapeDtypeStruct((n, d), F32),
        compiler_params=_params("arbitrary"),
        name="moe",
    )(h2d, comb, wg, wu, wd, g2, b2)


def _rope_angles(pos):
    half = HEAD_DIM // 2
    inv_freq = jnp.power(ROPE_THETA, -jnp.arange(half, dtype=F32) / half)
    ang = pos.astype(F32)[:, None] * inv_freq[None, :]
    return jnp.cos(ang), jnp.sin(ang)


def _rope_lane_tables(cos, sin):
    return (jnp.tile(jnp.concatenate([cos, cos], axis=-1), (1, N_HEADS)),
            jnp.tile(jnp.concatenate([-sin, sin], axis=-1), (1, N_HEADS)))


def _upper_ones(n):
    r = lax.broadcasted_iota(jnp.int32, (n, n), 0)
    c = lax.broadcasted_iota(jnp.int32, (n, n), 1)
    return (r <= c).astype(BF16)


def kernel(x_prompt, x_sample, cache_k_moba, cache_v_moba, cache_k_fox, cache_v_fox, cache_logf_fox,
           page_table, w_in, b_f, w_br_a, w_br_b, w_o, ln1_g, ln1_b, w_rg, b_rg, w_re, b_re,
           w_gate, w_up, w_down, ln2_g, ln2_b):
    batch, seq, d = x_prompt.shape
    bsz, t_new, _ = x_sample.shape
    depth = w_in.shape[0]
    n_pool = cache_k_moba.shape[1]
    n_pages = page_table.shape[1]
    past_len = n_pages * PAGE_SIZE
    alpha = (2 * depth) ** 0.25
    tm = MOBA_BLOCK
    n_s = bsz * t_new
    assert seq % tm == 0 and n_pages % PAGES_PER_STEP == 0 and t_new <= NEW_ROWS
    assert past_len // MOBA_BLOCK <= LANES and seq // MOBA_BLOCK <= LANES
    assert w_in.shape[2] == 6 * D_ATT + N_HEADS + 2 * d

    sec = lambda j: w_in[:, :, j * D_ATT:(j + 1) * D_ATT]
    w_in_t = jnp.swapaxes(w_in, 1, 2)
    sec_t = lambda j: w_in_t[:, j * D_ATT:(j + 1) * D_ATT, :]
    wn = jnp.concatenate([sec(0), sec(2), sec(3), sec(5)], axis=2).astype(BF16)
    wt = jnp.concatenate([sec_t(1), sec_t(2), sec_t(4), sec_t(5)], axis=1).astype(BF16)
    wa = w_in[:, :, :3 * D_ATT].astype(BF16)
    wb = w_in[:, :, 3 * D_ATT:6 * D_ATT].astype(BF16)
    wft = jnp.pad(w_in_t[:, 6 * D_ATT:6 * D_ATT + N_HEADS, :],
                  ((0, 0), (0, HEAD_ROWS - N_HEADS), (0, 0))).astype(BF16)
    bfp = jnp.pad(b_f.astype(F32), ((0, 0), (0, HEAD_ROWS - N_HEADS)))[:, :, None]
    wg = w_in[:, :, 6 * D_ATT + N_HEADS:].astype(BF16)
    wbra, wbrb, wo = w_br_a.astype(BF16), w_br_b.astype(BF16), w_o.astype(BF16)
    wr = jnp.pad(jnp.concatenate([w_re, w_rg], axis=-1).astype(F32),
                 ((0, 0), (0, 0), (0, LANES - N_EXPERTS - N_GROUPS)))
    wrh = wr.astype(BF16)
    wrl = (wr - wrh.astype(F32)).astype(BF16)
    br = jnp.pad(jnp.concatenate([b_re, b_rg], axis=-1).astype(F32),
                 ((0, 0), (0, LANES - N_EXPERTS - N_GROUPS)))[:, None, :]
    wgt, wup, wdn = w_gate.astype(BF16), w_up.astype(BF16), w_down.astype(BF16)
    row2 = lambda a: a.astype(F32)[:, None, :]
    g1, b1, g2, b2 = row2(ln1_g), row2(ln1_b), row2(ln2_g), row2(ln2_b)

    cos_p, sin_p = _rope_angles(jnp.arange(seq, dtype=jnp.int32))
    cos_pl, sin_pl = _rope_lane_tables(cos_p, sin_p)
    cos_pt, sin_pt = cos_p.T, sin_p.T
    cos_s, sin_s = _rope_lane_tables(*_rope_angles(past_len + jnp.arange(t_new, dtype=jnp.int32)))
    cos_s, sin_s = jnp.tile(cos_s, (bsz, 1)), jnp.tile(sin_s, (bsz, 1))
    tri_p, tri_page = _upper_ones(tm), _upper_ones(PAGE_SIZE)

    page_t = lambda c: jnp.transpose(c, (0, 1, 3, 4, 2)).reshape(depth, n_pool, D_ATT, PAGE_SIZE)
    ckt_moba, cvt_moba = page_t(cache_k_moba), page_t(cache_v_moba)
    ckt_fox, cvt_fox = page_t(cache_k_fox), page_t(cache_v_fox)
    c_lft = jnp.swapaxes(cache_logf_fox.astype(F32), 2, 3)

    hp = x_prompt.reshape(batch * seq, d)
    hs = x_sample.reshape(n_s, d)
    rows_p, rows_s = [], []
    for l in range(depth):
        (qa, vab, qb, vbb, kat, vat, kbt, vbt, katb, kbtb, sga, sgb, lft, ckt, kmean) = _proj_prompt(
            hp, wn[l], wt[l], wg[l], wft[l], bfp[l], cos_pl, sin_pl, cos_pt, sin_pt, tri_p, batch=batch, tm=tm)
        kmean_t = jnp.transpose(kmean.reshape(batch, seq // tm, D_ATT), (0, 2, 1))
        ya = _attn_prompt(qa, katb, vab, kmean_t, batch=batch, mode="moba")
        yb = _attn_prompt(qb, kbtb, vbb, ckt, batch=batch, mode="fox")
        h1, comb = _merge(hp, ya, yb, sga, sgb, wbra[l], wbrb[l], wo[l], g1[l], b1[l],
                          wrh[l], wrl[l], br[l], tm=tm, alpha=alpha)
        hp = _moe(h1, comb, wgt[l], wup[l], wdn[l], g2[l], b2[l], tm=tm, alpha=alpha)
        rows_t = lambda a: jnp.transpose(a.reshape(batch, N_HEADS, HEAD_DIM, seq), (0, 3, 1, 2))
        lf_p = jnp.transpose(lft[:N_HEADS].reshape(N_HEADS, batch, seq), (1, 2, 0))
        rows_p.append((rows_t(kat), rows_t(vat), rows_t(kbt), rows_t(vbt), lf_p))

        (qa, ka, va, qb, kb, vb, sga, sgb, lft) = _proj_decode(
            hs, wa[l], wb[l], wg[l], wft[l], bfp[l], cos_s, sin_s)
        pad_new = lambda a: jnp.pad(a.reshape(bsz, t_new, D_ATT), ((0, 0), (0, NEW_ROWS - t_new), (0, 0)))
        rep_q = lambda a: jnp.repeat(a.reshape(bsz, t_new, D_ATT), N_HEADS, axis=1)
        lf_s = lft[:N_HEADS].reshape(N_HEADS, bsz, t_new)
        lf_new = jnp.pad(jnp.transpose(lf_s, (1, 0, 2)), ((0, 0), (0, 0), (0, NEW_ROWS - t_new)))
        ya = _attn_decode(rep_q(qa), pad_new(ka), pad_new(va), None, page_table, ckt_moba, cvt_moba, None,
                          None, layer=l, mode="moba")
        yb = _attn_decode(rep_q(qb), pad_new(kb), pad_new(vb), lf_new, page_table, ckt_fox, cvt_fox, c_lft,
                          tri_page, layer=l, mode="fox")
        h1, comb = _merge(hs, ya.reshape(n_s, D_ATT), yb.reshape(n_s, D_ATT), sga, sgb, wbra[l], wbrb[l],
                          wo[l], g1[l], b1[l], wrh[l], wrl[l], br[l], tm=n_s, alpha=alpha)
        hs = _moe(h1, comb, wgt[l], wup[l], wdn[l], g2[l], b2[l], tm=n_s, alpha=alpha)
        rows_s4 = lambda a: a.reshape(bsz, t_new, N_HEADS, HEAD_DIM)
        rows_s.append((rows_s4(ka), rows_s4(va), rows_s4(kb), rows_s4(vb), jnp.transpose(lf_s, (1, 2, 0))))

    stack = lambda rows, idx: jnp.stack([r[idx] for r in rows], axis=0)
    return (hp.reshape(batch, seq, d), hs.reshape(bsz, t_new, d),
            *[stack(rows_p, i) for i in range(5)],
            *[stack(rows_s, i) for i in range(5)])
```

```python
import functools

import jax
import jax.numpy as jnp
from jax import lax
from jax.experimental import pallas as pl
from jax.experimental.pallas import tpu as pltpu

F32 = jnp.float32
BF16 = jnp.bfloat16

HEAD_DIM = 64
N_HEADS = 8
D_ATT = N_HEADS * HEAD_DIM
MOBA_BLOCK = 256
MOBA_TOPK = 3
PAGE_SIZE = 128
ROPE_THETA = 10000.0
N_GROUPS = 4
EXPERTS_PER_GROUP = 4
N_EXPERTS = N_GROUPS * EXPERTS_PER_GROUP
LN_EPS = 1e-5

LANES = 128
HEAD_ROWS = 16
NEW_ROWS = 16
PAGES_PER_STEP = 16
ATTN_Q_TILE = 512
ATTN_K_TILE = 1024
MASKED = -1e30
VMEM_LIMIT = 56 * 1024 * 1024


def _dot(a, b):
    return jnp.dot(a, b, preferred_element_type=F32)


def _dot_nt(a, b):
    return lax.dot_general(a, b, (((1,), (1,)), ((), ())), preferred_element_type=F32)


def _split3(x):
    hi = x.astype(BF16)
    r1 = x - hi.astype(F32)
    mid = r1.astype(BF16)
    lo = (r1 - mid.astype(F32)).astype(BF16)
    return hi, mid, lo


def _layer_norm(z, g, b):
    mu = jnp.mean(z, axis=-1, keepdims=True)
    zc = z - mu
    var = jnp.mean(zc * zc, axis=-1, keepdims=True)
    return zc * lax.rsqrt(var + LN_EPS) * g + b


def _log_sigmoid(z):
    return jnp.minimum(z, 0.0) - jnp.log1p(jnp.exp(-jnp.abs(z)))


def _const_spec(shape):
    nd = len(shape)
    return pl.BlockSpec(shape, lambda *_: (0,) * nd, pipeline_mode=pl.Buffered(1))


def _params(*sem):
    return pltpu.CompilerParams(dimension_semantics=sem, vmem_limit_bytes=VMEM_LIMIT)


def _rope_lanes(t, cos, sin):
    lane = lax.broadcasted_iota(jnp.int32, t.shape, 1)
    first_half = (lane & (HEAD_DIM - 1)) < (HEAD_DIM // 2)
    swapped = jnp.where(first_half,
                        pltpu.roll(t, t.shape[1] - HEAD_DIM // 2, 1),
                        pltpu.roll(t, HEAD_DIM // 2, 1))
    return t * cos + swapped * sin


def _rope_rows(t, cos, sin):
    half = HEAD_DIM // 2
    out = []
    for h in range(N_HEADS):
        x1 = t[h * HEAD_DIM:h * HEAD_DIM + half]
        x2 = t[h * HEAD_DIM + half:(h + 1) * HEAD_DIM]
        out += [x1 * cos - x2 * sin, x2 * cos + x1 * sin]
    return jnp.concatenate(out, axis=0)


def _forget_rows(xb, wft_ref, bf_ref):
    return _log_sigmoid(_dot_nt(wft_ref[...], xb) + bf_ref[...])


def _pad_heads(x, fill):
    lane = lax.broadcasted_iota(jnp.int32, (x.shape[0], LANES), 1)
    cols = []
    for h in range(N_HEADS):
        col = x[:, (h // 2) * LANES:(h // 2 + 1) * LANES]
        if h % 2:
            col = pltpu.roll(col, HEAD_DIM, 1)
        cols.append(jnp.where(lane < HEAD_DIM, col, fill(h, lane)))
    return jnp.concatenate(cols, axis=1)


def _aug_rows(kt, extra):
    return jnp.stack([jnp.concatenate([kt[h * HEAD_DIM:(h + 1) * HEAD_DIM], extra(h)], axis=0)
                      for h in range(N_HEADS)], axis=0)


def _ones_lane(h, lane):
    return jnp.where(lane == HEAD_DIM, 1.0, 0.0)


def _fox_q_lanes(h, lane):
    off = lane - HEAD_DIM - h
    return jnp.where((off == 0) | (off == HEAD_ROWS) | (off == 2 * HEAD_ROWS), 1.0, 0.0)


def _proj_prompt_kernel(x_ref, wn_ref, wt_ref, wg_ref, wft_ref, bf_ref, cos_ref, sin_ref, cost_ref, sint_ref,
                        tri_ref, qa_ref, vaa_ref, qb_ref, vba_ref, kat_ref, vat_ref, kbt_ref, vbt_ref,
                        kaa_ref, kba_ref, sga_ref, sgb_ref, lft_ref, kmean_ref, carry_ref,
                        *, tiles_per_seq):
    i = pl.program_id(0)
    tm = x_ref.shape[0]
    xb = x_ref[...].astype(BF16)
    scale = HEAD_DIM ** -0.5
    no_fill = lambda h, lane: 0.0

    pn = _dot(xb, wn_ref[...])
    qa = _rope_lanes(pn[:, :D_ATT], cos_ref[...], sin_ref[...]) * scale
    qa_ref[...] = _pad_heads(qa, no_fill).astype(BF16)
    vaa_ref[...] = _pad_heads(pn[:, D_ATT:2 * D_ATT], _ones_lane).astype(BF16)
    qb_ref[...] = _pad_heads(pn[:, 2 * D_ATT:3 * D_ATT] * scale, _fox_q_lanes).astype(BF16)
    vba_ref[...] = _pad_heads(pn[:, 3 * D_ATT:], _ones_lane).astype(BF16)

    lft = _forget_rows(xb, wft_ref, bf_ref)
    lft_ref[...] = lft

    @pl.when(i % tiles_per_seq == 0)
    def _():
        carry_ref[...] = jnp.zeros_like(carry_ref)

    hi, mid, lo = _split3(lft)
    tri = tri_ref[...]
    ck = _dot(hi, tri) + _dot(mid, tri) + _dot(lo, tri) + carry_ref[...]
    carry_ref[...] = ck[:, tm - 1:tm]

    pt = _dot_nt(wt_ref[...], xb)
    kat = _rope_rows(pt[:D_ATT], cost_ref[...], sint_ref[...])
    kat_ref[...] = kat
    kmean = jnp.mean(kat, axis=1, keepdims=True)
    kmean_ref[...] = _aug_rows(kmean, lambda h: jnp.zeros((LANES - HEAD_DIM, 1), F32))
    blk_row = lax.broadcasted_iota(jnp.int32, (LANES - HEAD_DIM, tm), 0)
    blk_ind = jnp.where(blk_row == i % tiles_per_seq, 1.0, 0.0)
    kaa_ref[...] = _aug_rows(kat, lambda h: blk_ind).astype(BF16)
    vat_ref[...] = pt[D_ATT:2 * D_ATT]
    kbt = pt[2 * D_ATT:3 * D_ATT]
    kbt_ref[...] = kbt
    chi, cmid, clo = _split3(ck)
    bias_rows = -jnp.concatenate([chi.astype(F32), cmid.astype(F32), clo.astype(F32),
                                  jnp.zeros((LANES - HEAD_DIM - 3 * HEAD_ROWS, tm), F32)], axis=0)
    kba_ref[...] = _aug_rows(kbt, lambda h: bias_rows).astype(BF16)
    vbt_ref[...] = pt[3 * D_ATT:]

    pg = _dot(xb, wg_ref[...])
    d_model = pg.shape[1] // 2
    sga_ref[...] = jax.nn.sigmoid(pg[:, :d_model])
    sgb_ref[...] = jax.nn.sigmoid(pg[:, d_model:])


def _proj_prompt(x2d, wn, wt, wg, wft, bfp, cos, sin, cost, sint, tri, *, batch, tm):
    n, d = x2d.shape
    t = n // batch
    tps = t // tm
    nt = n // tm
    aug = N_HEADS * LANES
    row = lambda w: pl.BlockSpec((tm, w), lambda i: (i, 0))
    tab = pl.BlockSpec((tm, D_ATT), lambda i: (i % tps, 0))
    tabt = pl.BlockSpec((HEAD_DIM // 2, tm), lambda i: (0, i % tps))
    heads_t = pl.BlockSpec((HEAD_ROWS, tm), lambda i: (0, i))
    trans = pl.BlockSpec((None, D_ATT, tm), lambda i: (i // tps, 0, i % tps))
    trans_aug = pl.BlockSpec((None, N_HEADS, LANES, tm), lambda i: (i // tps, 0, 0, i % tps))
    out_shape = ([jax.ShapeDtypeStruct((n, aug), BF16)] * 4
                 + [jax.ShapeDtypeStruct((batch, D_ATT, t), F32)] * 4
                 + [jax.ShapeDtypeStruct((batch, N_HEADS, LANES, t), BF16)] * 2
                 + [jax.ShapeDtypeStruct((n, d), F32)] * 2
                 + [jax.ShapeDtypeStruct((HEAD_ROWS, n), F32)]
                 + [jax.ShapeDtypeStruct((nt, N_HEADS, LANES, 1), F32)])
    out_specs = ([row(aug)] * 4 + [trans] * 4 + [trans_aug] * 2 + [row(d)] * 2 + [heads_t]
                 + [pl.BlockSpec((None, N_HEADS, LANES, 1), lambda i: (i, 0, 0, 0))])
    return pl.pallas_call(
        functools.partial(_proj_prompt_kernel, tiles_per_seq=tps),
        grid=(nt,),
        in_specs=[row(d), _const_spec(wn.shape), _const_spec(wt.shape), _const_spec(wg.shape),
                  _const_spec(wft.shape), _const_spec(bfp.shape), tab, tab, tabt, tabt,
                  _const_spec(tri.shape)],
        out_specs=out_specs,
        out_shape=out_shape,
        scratch_shapes=[pltpu.VMEM((HEAD_ROWS, 1), F32)],
        compiler_params=_params("arbitrary"),
        name="proj_prompt",
    )(x2d, wn, wt, wg, wft, bfp, cos, sin, cost, sint, tri)


def _proj_decode_kernel(x_ref, wa_ref, wb_ref, wg_ref, wft_ref, bf_ref, cos_ref, sin_ref,
                        qa_ref, ka_ref, va_ref, qb_ref, kb_ref, vb_ref, sga_ref, sgb_ref, lft_ref):
    xb = x_ref[...].astype(BF16)
    scale = HEAD_DIM ** -0.5
    cos, sin = cos_ref[...], sin_ref[...]
    pa = _dot(xb, wa_ref[...])
    qa_ref[...] = (_rope_lanes(pa[:, :D_ATT], cos, sin) * scale).astype(BF16)
    ka_ref[...] = _rope_lanes(pa[:, D_ATT:2 * D_ATT], cos, sin)
    va_ref[...] = pa[:, 2 * D_ATT:]
    pb = _dot(xb, wb_ref[...])
    qb_ref[...] = (pb[:, :D_ATT] * scale).astype(BF16)
    kb_ref[...] = pb[:, D_ATT:2 * D_ATT]
    vb_ref[...] = pb[:, 2 * D_ATT:]
    pg = _dot(xb, wg_ref[...])
    d_model = pg.shape[1] // 2
    sga_ref[...] = jax.nn.sigmoid(pg[:, :d_model])
    sgb_ref[...] = jax.nn.sigmoid(pg[:, d_model:])
    lft_ref[...] = _forget_rows(xb, wft_ref, bf_ref)


def _proj_decode(x2d, wa, wb, wg, wft, bfp, cos, sin):
    n, d = x2d.shape
    full = lambda a: pl.BlockSpec(a.shape, lambda i: (0,) * a.ndim)
    ins = (x2d, wa, wb, wg, wft, bfp, cos, sin)
    out_shape = ([jax.ShapeDtypeStruct((n, D_ATT), BF16)] + [jax.ShapeDtypeStruct((n, D_ATT), F32)] * 2
                 + [jax.ShapeDtypeStruct((n, D_ATT), BF16)] + [jax.ShapeDtypeStruct((n, D_ATT), F32)] * 2
                 + [jax.ShapeDtypeStruct((n, d), F32)] * 2
                 + [jax.ShapeDtypeStruct((HEAD_ROWS, n), F32)])
    return pl.pallas_call(
        _proj_decode_kernel,
        grid=(1,),
        in_specs=[full(a) for a in ins],
        out_specs=[full(s) for s in out_shape],
        out_shape=out_shape,
        compiler_params=_params("arbitrary"),
        name="proj_decode",
    )(*ins)


def _attn_kernel(q_ref, kt_ref, v_ref, *rest, mode, tk):
    o_ref = rest[-1]
    i = pl.program_id(2)
    tq = q_ref.shape[0]
    n_full = (i * tq) // tk
    off_d = pl.multiple_of(n_full * tk, tk)
    rows = i * tq + lax.broadcasted_iota(jnp.int32, (tq, tk), 0)
    cols = off_d + lax.broadcasted_iota(jnp.int32, (tq, tk), 1)
    causal = cols <= rows

    qs = []
    for hh in range(2):
        q = q_ref[:, hh * LANES:(hh + 1) * LANES]
        if mode == "moba":
            km_ref = rest[0]
            nb = km_ref.shape[1]
            gate = _dot_nt(km_ref[hh], q)
            blk = lax.broadcasted_iota(jnp.int32, (nb, tq), 0)
            own = (i * tq + lax.broadcasted_iota(jnp.int32, (nb, tq), 1)) // MOBA_BLOCK
            g = jnp.where(blk < own, gate, -jnp.inf)
            keep = blk == own
            for _ in range(min(MOBA_TOPK, nb)):
                mx = jnp.max(g, axis=0, keepdims=True)
                first = jnp.min(jnp.where(g == mx, blk, nb), axis=0, keepdims=True)
                pick = (blk == first) & (mx > -jnp.inf)
                keep = keep | pick
                g = jnp.where(pick, -jnp.inf, g)
            bias_t = jnp.concatenate([jnp.zeros((HEAD_DIM, tq), F32), jnp.where(keep, 0.0, MASKED),
                                      jnp.zeros((LANES - HEAD_DIM - nb, tq), F32)], axis=0)
            q = (q.astype(F32) + bias_t.T).astype(BF16)
        qs.append(q)

    def chunk(off, diagonal, state):
        new = []
        for hh in range(2):
            m, acc = state[hh]
            s = _dot(qs[hh], kt_ref[hh, :, pl.ds(off, tk)])
            if diagonal:
                s = jnp.where(causal, s, MASKED)
            m_new = jnp.maximum(m, jnp.max(s, axis=1, keepdims=True))
            p = jnp.exp(s - m_new).astype(BF16)
            acc = jnp.exp(m - m_new) * acc + _dot(p, v_ref[pl.ds(off, tk), hh * LANES:(hh + 1) * LANES])
            new.append((m_new, acc))
        return tuple(new)

    init = tuple((jnp.full((tq, 1), MASKED, F32), jnp.zeros((tq, LANES), F32)) for _ in range(2))
    state = chunk(off_d, True, init)
    state = lax.fori_loop(0, n_full, lambda c, st: chunk(pl.multiple_of(c * tk, tk), False, st), state)

    res = [acc / acc[:, HEAD_DIM:HEAD_DIM + 1] for _, acc in state]
    lane = lax.broadcasted_iota(jnp.int32, (tq, LANES), 1)
    o_ref[...] = jnp.where(lane < HEAD_DIM, res[0], pltpu.roll(res[1], HEAD_DIM, 1)).astype(o_ref.dtype)


def _attn_prompt(q, kt, v, kmean, *, batch, mode):
    n = q.shape[0]
    t = n // batch
    tq, tk = ATTN_Q_TILE, ATTN_K_TILE
    assert t % tk == 0 and tk % tq == 0 and tq % MOBA_BLOCK == 0
    q3, v3 = q.reshape(batch, t, -1), v.reshape(batch, t, -1)
    in_specs = [pl.BlockSpec((None, tq, 2 * LANES), lambda b, hp, i: (b, i, hp)),
                pl.BlockSpec((None, 2, LANES, t), lambda b, hp, i: (b, hp, 0, 0)),
                pl.BlockSpec((None, t, 2 * LANES), lambda b, hp, i: (b, 0, hp))]
    args = [q3, kt, v3]
    if mode == "moba":
        assert kmean.shape[2] <= LANES - HEAD_DIM
        in_specs.append(pl.BlockSpec((None, 2, kmean.shape[2], LANES), lambda b, hp, i: (b, hp, 0, 0)))
        args.append(kmean)
    out = pl.pallas_call(
        functools.partial(_attn_kernel, mode=mode, tk=tk),
        grid=(batch, N_HEADS // 2, t // tq),
        in_specs=in_specs,
        out_specs=pl.BlockSpec((None, tq, LANES), lambda b, hp, i: (b, i, hp)),
        out_shape=jax.ShapeDtypeStruct((batch, t, D_ATT), BF16),
        compiler_params=_params("arbitrary", "arbitrary", "arbitrary"),
        name="attn_" + mode,
    )(*args)
    return out.reshape(n, D_ATT)


def _head_rows_q(q_ref):
    q = q_ref[...]
    r = lax.broadcasted_iota(jnp.int32, q.shape, 0)
    c = lax.broadcasted_iota(jnp.int32, q.shape, 1)
    own = (c // HEAD_DIM) == (r % N_HEADS)
    return jnp.where(own, q, jnp.zeros_like(q)), own


def _head_diag(res, own, t_new):
    kept = jnp.where(own, res, 0.0)
    return jnp.concatenate(
        [jnp.sum(kept[t * N_HEADS:(t + 1) * N_HEADS], axis=0, keepdims=True) for t in range(t_new)], axis=0)


def _sfox_kernel(pt_ref, q_ref, *refs, n_steps, t_new):
    del pt_ref
    npg = PAGES_PER_STEP
    k_refs, v_refs, lf_refs = refs[:npg], refs[npg:2 * npg], refs[2 * npg:3 * npg]
    knew_ref, vnew_ref, lfnew_ref, tri_ref, o_ref, m_scr, l_scr, acc_scr, run_scr = refs[3 * npg:]
    c = pl.program_id(1)
    qbig, own = _head_rows_q(q_ref)

    @pl.when(c == 0)
    def _():
        m_scr[...] = jnp.full_like(m_scr, MASKED)
        l_scr[...] = jnp.zeros_like(l_scr)
        acc_scr[...] = jnp.zeros_like(acc_scr)
        run_scr[...] = jnp.zeros_like(run_scr)

    kt_all = jnp.concatenate([r[...] for r in k_refs], axis=1).astype(BF16)
    vt_all = jnp.concatenate([r[...] for r in v_refs], axis=1).astype(BF16)
    s = _dot(qbig, kt_all)

    lf = jnp.concatenate([r[...] for r in lf_refs], axis=0)
    hi, mid, lo = _split3(lf)
    tri = tri_ref[...]
    local = _dot(hi, tri) + _dot(mid, tri) + _dot(lo, tri)
    run = run_scr[...]
    pieces = []
    for pg in range(npg):
        cp = local[pg * N_HEADS:(pg + 1) * N_HEADS] + run
        pieces.append(cp)
        run = cp[:, PAGE_SIZE - 1:PAGE_SIZE]
    run_scr[...] = run
    ck = jnp.concatenate(pieces, axis=1)
    s = s - jnp.concatenate([ck] * t_new, axis=0)

    m_old = m_scr[...]
    m_new = jnp.maximum(m_old, jnp.max(s, axis=1, keepdims=True))
    a = jnp.exp(m_old - m_new)
    p = jnp.exp(s - m_new)
    l_new = a * l_scr[...] + jnp.sum(p, axis=1, keepdims=True)
    acc_new = a * acc_scr[...] + _dot_nt(p.astype(BF16), vt_all)
    m_scr[...] = m_new
    l_scr[...] = l_new
    acc_scr[...] = acc_new

    @pl.when(c == n_steps - 1)
    def _():
        sn = _dot_nt(qbig, knew_ref[...].astype(BF16))
        lfn = lfnew_ref[...]
        u8 = lax.broadcasted_iota(jnp.int32, lfn.shape, 1)
        cn = run + jnp.zeros_like(lfn)
        for u in range(t_new):
            cn = cn + jnp.where(u8 >= u, lfn[:, u:u + 1], 0.0)
        sn = sn - jnp.concatenate([cn] * t_new, axis=0)
        u = lax.broadcasted_iota(jnp.int32, sn.shape, 1)
        tok = lax.broadcasted_iota(jnp.int32, sn.shape, 0) // N_HEADS
        sn = jnp.where(u <= tok, sn, MASKED)
        m_fin = jnp.maximum(m_new, jnp.max(sn, axis=1, keepdims=True))
        a2 = jnp.exp(m_new - m_fin)
        pn = jnp.exp(sn - m_fin)
        l_fin = a2 * l_new + jnp.sum(pn, axis=1, keepdims=True)
        acc_fin = a2 * acc_new + _dot(pn.astype(BF16), vnew_ref[...].astype(BF16))
        o_ref[...] = _head_diag(acc_fin / l_fin, own, t_new).astype(o_ref.dtype)


def _smoba_kernel(pt_ref, q_ref, *refs, n_steps, t_new):
    del pt_ref
    npg = PAGES_PER_STEP
    k_refs, v_refs = refs[:npg], refs[npg:2 * npg]
    knew_ref, vnew_ref, o_ref, m_scr, l_scr, g_scr, acc_scr = refs[2 * npg:]
    c = pl.program_id(1)
    qbig, own = _head_rows_q(q_ref)
    nrow = qbig.shape[0]
    blocks_per_step = npg * PAGE_SIZE // MOBA_BLOCK
    n_past = n_steps * blocks_per_step
    blk_lane = lax.broadcasted_iota(jnp.int32, (nrow, LANES), 1)

    @pl.when(c == 0)
    def _():
        m_scr[...] = jnp.zeros_like(m_scr)
        l_scr[...] = jnp.zeros_like(l_scr)
        g_scr[...] = jnp.zeros_like(g_scr)

    kt_all = jnp.concatenate([r[...] for r in k_refs], axis=1).astype(BF16)
    vt_all = jnp.concatenate([r[...] for r in v_refs], axis=1).astype(BF16)
    s = _dot(qbig, kt_all)
    m_all, l_all, g_all = m_scr[...], l_scr[...], g_scr[...]
    for jb in range(blocks_per_step):
        sb = s[:, jb * MOBA_BLOCK:(jb + 1) * MOBA_BLOCK]
        gb = jnp.sum(sb, axis=1, keepdims=True)
        mb = jnp.max(sb, axis=1, keepdims=True)
        p = jnp.exp(sb - mb)
        lb = jnp.sum(p, axis=1, keepdims=True)
        blk = c * blocks_per_step + jb
        acc_scr[blk] = _dot_nt(p.astype(BF16), vt_all[:, jb * MOBA_BLOCK:(jb + 1) * MOBA_BLOCK])
        hit = blk_lane == blk
        m_all = jnp.where(hit, mb, m_all)
        l_all = jnp.where(hit, lb, l_all)
        g_all = jnp.where(hit, gb, g_all)
    m_scr[...] = m_all
    l_scr[...] = l_all
    g_scr[...] = g_all

    @pl.when(c == n_steps - 1)
    def _():
        g = jnp.where(blk_lane < n_past, g_all, -jnp.inf)
        sel = blk_lane < 0
        for _ in range(min(MOBA_TOPK, n_past)):
            mx = jnp.max(g, axis=1, keepdims=True)
            first = jnp.min(jnp.where(g == mx, blk_lane, LANES), axis=1, keepdims=True)
            pick = (blk_lane == first) & (mx > -jnp.inf)
            sel = sel | pick
            g = jnp.where(pick, -jnp.inf, g)

        sn = _dot_nt(qbig, knew_ref[...].astype(BF16))
        u = lax.broadcasted_iota(jnp.int32, sn.shape, 1)
        tok = lax.broadcasted_iota(jnp.int32, sn.shape, 0) // N_HEADS
        sn = jnp.where(u <= tok, sn, MASKED)
        m_own = jnp.max(sn, axis=1, keepdims=True)
        p_own = jnp.exp(sn - m_own)
        l_own = jnp.sum(p_own, axis=1, keepdims=True)
        acc_own = _dot(p_own.astype(BF16), vnew_ref[...].astype(BF16))

        m_fin = jnp.maximum(jnp.max(jnp.where(sel, m_all, MASKED), axis=1, keepdims=True), m_own)
        w = jnp.where(sel, jnp.exp(m_all - m_fin), 0.0)
        w_own = jnp.exp(m_own - m_fin)
        l_fin = jnp.sum(jnp.where(sel, w * l_all, 0.0), axis=1, keepdims=True) + w_own * l_own

        def body(j, out):
            wj = jnp.sum(jnp.where(blk_lane == j, w, 0.0), axis=1, keepdims=True)
            return out + wj * acc_scr[j]

        out = lax.fori_loop(0, n_past, body, w_own * acc_own)
        o_ref[...] = _head_diag(out / l_fin, own, t_new).astype(o_ref.dtype)


def _attn_decode(q, k_new, v_new, lf_new, page_table, cache_kt, cache_vt, cache_lft, tri, *, layer, mode):
    bsz, nrow, _ = q.shape
    t_new = nrow // N_HEADS
    n_pages = page_table.shape[1]
    npg = PAGES_PER_STEP
    n_steps = n_pages // npg
    pt_flat = page_table.reshape(-1)

    def page_spec(pg, rows):
        return pl.BlockSpec((None, None, rows, PAGE_SIZE),
                            lambda b, c, pt: (layer, pt[b * n_pages + c * npg + pg], 0, 0))

    per_seq = lambda rows, width: pl.BlockSpec((None, rows, width), lambda b, c, pt: (b, 0, 0))
    kv_specs = [page_spec(pg, D_ATT) for pg in range(npg)]
    in_specs = [per_seq(nrow, D_ATT)] + kv_specs + kv_specs
    args = [q] + [cache_kt] * npg + [cache_vt] * npg
    if mode == "fox":
        in_specs += [page_spec(pg, N_HEADS) for pg in range(npg)]
        args += [cache_lft] * npg
        in_specs += [per_seq(NEW_ROWS, D_ATT), per_seq(NEW_ROWS, D_ATT), per_seq(N_HEADS, NEW_ROWS),
                     pl.BlockSpec(tri.shape, lambda b, c, pt: (0, 0))]
        args += [k_new, v_new, lf_new, tri]
        scratch = [pltpu.VMEM((nrow, 1), F32), pltpu.VMEM((nrow, 1), F32),
                   pltpu.VMEM((nrow, D_ATT), F32), pltpu.VMEM((N_HEADS, 1), F32)]
        body = _sfox_kernel
    else:
        in_specs += [per_seq(NEW_ROWS, D_ATT), per_seq(NEW_ROWS, D_ATT)]
        args += [k_new, v_new]
        n_past = n_steps * npg * PAGE_SIZE // MOBA_BLOCK
        scratch = [pltpu.VMEM((nrow, LANES), F32)] * 3 + [pltpu.VMEM((n_past, nrow, D_ATT), F32)]
        body = _smoba_kernel
    return pl.pallas_call(
        functools.partial(body, n_steps=n_steps, t_new=t_new),
        grid_spec=pltpu.PrefetchScalarGridSpec(
            num_scalar_prefetch=1, grid=(bsz, n_steps), in_specs=in_specs,
            out_specs=pl.BlockSpec((None, t_new, D_ATT), lambda b, c, pt: (b, 0, 0)),
            scratch_shapes=scratch),
        out_shape=jax.ShapeDtypeStruct((bsz, t_new, D_ATT), BF16),
        compiler_params=_params("arbitrary", "arbitrary"),
        name="decode_" + mode,
    )(pt_flat, *args)


def _merge_kernel(x_ref, ya_ref, yb_ref, sga_ref, sgb_ref, wa_ref, wb_ref, wo_ref, g1_ref, b1_ref,
                  wrh_ref, wrl_ref, br_ref, h_ref, comb_ref, *, alpha):
    ua = _dot(ya_ref[...], wa_ref[...])
    ub = _dot(yb_ref[...], wb_ref[...])
    merged = sga_ref[...] * ua + sgb_ref[...] * ub
    mix = _dot(merged.astype(BF16), wo_ref[...])
    h = _layer_norm(alpha * x_ref[...] + mix, g1_ref[...], b1_ref[...])
    h_ref[...] = h

    h_hi = h.astype(BF16)
    h_lo = (h - h_hi.astype(F32)).astype(BF16)
    wrh = wrh_ref[...]
    logit = _dot(h_hi, wrh) + _dot(h_lo, wrh) + _dot(h_hi, wrl_ref[...]) + br_ref[...]
    lane = lax.broadcasted_iota(jnp.int32, logit.shape, 1)
    is_group = (lane >= N_EXPERTS) & (lane < N_EXPERTS + N_GROUPS)
    g_max = jnp.max(jnp.where(is_group, logit, -jnp.inf), axis=1, keepdims=True)
    g_sel = jnp.min(jnp.where(is_group & (logit == g_max), lane - N_EXPERTS, LANES), axis=1, keepdims=True)
    g_sum = jnp.sum(jnp.where(is_group, jnp.exp(logit - g_max), 0.0), axis=1, keepdims=True)
    p_g = 1.0 / g_sum
    in_group = (lane < N_EXPERTS) & ((lane // EXPERTS_PER_GROUP) == g_sel)
    e_max = jnp.max(jnp.where(in_group, logit, -jnp.inf), axis=1, keepdims=True)
    e_exp = jnp.where(in_group, jnp.exp(logit - e_max), 0.0)
    e_prob = e_exp / jnp.sum(e_exp, axis=1, keepdims=True)
    p1 = jnp.max(jnp.where(in_group, e_prob, -1.0), axis=1, keepdims=True)
    i1 = jnp.min(jnp.where(in_group & (e_prob == p1), lane, LANES), axis=1, keepdims=True)
    rest = in_group & (lane != i1)
    p2 = jnp.max(jnp.where(rest, e_prob, -1.0), axis=1, keepdims=True)
    i2 = jnp.min(jnp.where(rest & (e_prob == p2), lane, LANES), axis=1, keepdims=True)
    den = p1 + p2
    comb_ref[...] = p_g * jnp.where(lane == i1, p1 / den, jnp.where(lane == i2, p2 / den, 0.0))


def _merge(x2d, ya, yb, sga, sgb, wa, wb, wo, g1, b1, wrh, wrl, br, *, tm, alpha):
    n, d = x2d.shape
    row = lambda w: pl.BlockSpec((tm, w), lambda i: (i, 0))
    return pl.pallas_call(
        functools.partial(_merge_kernel, alpha=alpha),
        grid=(n // tm,),
        in_specs=[row(d), row(D_ATT), row(D_ATT), row(d), row(d),
                  _const_spec(wa.shape), _const_spec(wb.shape), _const_spec(wo.shape),
                  _const_spec(g1.shape), _const_spec(b1.shape),
                  _const_spec(wrh.shape), _const_spec(wrl.shape), _const_spec(br.shape)],
        out_specs=[row(d), row(LANES)],
        out_shape=[jax.ShapeDtypeStruct((n, d), F32), jax.ShapeDtypeStruct((n, LANES), F32)],
        compiler_params=_params("arbitrary"),
        name="merge",
    )(x2d, ya, yb, sga, sgb, wa, wb, wo, g1, b1, wrh, wrl, br)


def _moe_kernel(h_ref, comb_ref, wg_ref, wu_ref, wd_ref, g2_ref, b2_ref, o_ref, *, alpha):
    h = h_ref[...]
    hb = h.astype(BF16)
    comb = comb_ref[...]
    acc = jnp.zeros_like(h)
    for e in range(N_EXPERTS):
        g = _dot(hb, wg_ref[e])
        u = _dot(hb, wu_ref[e])
        act = g * jax.nn.sigmoid(g) * u * comb[:, e:e + 1]
        acc = acc + _dot(act.astype(BF16), wd_ref[e])
    o_ref[...] = _layer_norm(alpha * h + acc, g2_ref[...], b2_ref[...])


def _moe(h2d, comb, wg, wu, wd, g2, b2, *, tm, alpha):
    n, d = h2d.shape
    row = lambda w: pl.BlockSpec((tm, w), lambda i: (i, 0))
    return pl.pallas_call(
        functools.partial(_moe_kernel, alpha=alpha),
        grid=(n // tm,),
        in_specs=[row(d), row(LANES), _const_spec(wg.shape), _const_spec(wu.shape), _const_spec(wd.shape),
                  _const_spec(g2.shape), _const_spec(b2.shape)],
        out_specs=row(d),
        out_shape=jax.ShapeDtypeStruct((n, d), F32),
        compiler_params=_params("arbitrary"),
        name="moe",
    )(h2d, comb, wg, wu, wd, g2, b2)


def _rope_angles(pos):
    half = HEAD_DIM // 2
    inv_freq = jnp.power(ROPE_THETA, -jnp.arange(half, dtype=F32) / half)
    ang = pos.astype(F32)[:, None] * inv_freq[None, :]
    return jnp.cos(ang), jnp.sin(ang)


def _rope_lane_tables(cos, sin):
    return (jnp.tile(jnp.concatenate([cos, cos], axis=-1), (1, N_HEADS)),
            jnp.tile(jnp.concatenate([-sin, sin], axis=-1), (1, N_HEADS)))


def _upper_ones(n):
    r = lax.broadcasted_iota(jnp.int32, (n, n), 0)
    c = lax.broadcasted_iota(jnp.int32, (n, n), 1)
    return (r <= c).astype(BF16)


def kernel(x_prompt, x_sample, cache_k_moba, cache_v_moba, cache_k_fox, cache_v_fox, cache_logf_fox,
           page_table, w_in, b_f, w_br_a, w_br_b, w_o, ln1_g, ln1_b, w_rg, b_rg, w_re, b_re,
           w_gate, w_up, w_down, ln2_g, ln2_b):
    batch, seq, d = x_prompt.shape
    bsz, t_new, _ = x_sample.shape
    depth = w_in.shape[0]
    n_pool = cache_k_moba.shape[1]
    n_pages = page_table.shape[1]
    past_len = n_pages * PAGE_SIZE
    alpha = (2 * depth) ** 0.25
    tm = MOBA_BLOCK
    n_s = bsz * t_new
    assert seq % tm == 0 and n_pages % PAGES_PER_STEP == 0 and t_new <= NEW_ROWS
    assert past_len // MOBA_BLOCK <= LANES and seq // MOBA_BLOCK <= LANES
    assert w_in.shape[2] == 6 * D_ATT + N_HEADS + 2 * d

    sec = lambda j: w_in[:, :, j * D_ATT:(j + 1) * D_ATT]
    w_in_t = jnp.swapaxes(w_in, 1, 2)
    sec_t = lambda j: w_in_t[:, j * D_ATT:(j + 1) * D_ATT, :]
    wn = jnp.concatenate([sec(0), sec(2), sec(3), sec(5)], axis=2).astype(BF16)
    wt = jnp.concatenate([sec_t(1), sec_t(2), sec_t(4), sec_t(5)], axis=1).astype(BF16)
    wa = w_in[:, :, :3 * D_ATT].astype(BF16)
    wb = w_in[:, :, 3 * D_ATT:6 * D_ATT].astype(BF16)
    wft = jnp.pad(w_in_t[:, 6 * D_ATT:6 * D_ATT + N_HEADS, :],
                  ((0, 0), (0, HEAD_ROWS - N_HEADS), (0, 0))).astype(BF16)
    bfp = jnp.pad(b_f.astype(F32), ((0, 0), (0, HEAD_ROWS - N_HEADS)))[:, :, None]
    wg = w_in[:, :, 6 * D_ATT + N_HEADS:].astype(BF16)
    wbra, wbrb, wo = w_br_a.astype(BF16), w_br_b.astype(BF16), w_o.astype(BF16)
    wr = jnp.pad(jnp.concatenate([w_re, w_rg], axis=-1).astype(F32),
                 ((0, 0), (0, 0), (0, LANES - N_EXPERTS - N_GROUPS)))
    wrh = wr.astype(BF16)
    wrl = (wr - wrh.astype(F32)).astype(BF16)
    br = jnp.pad(jnp.concatenate([b_re, b_rg], axis=-1).astype(F32),
                 ((0, 0), (0, LANES - N_EXPERTS - N_GROUPS)))[:, None, :]
    wgt, wup, wdn = w_gate.astype(BF16), w_up.astype(BF16), w_down.astype(BF16)
    row2 = lambda a: a.astype(F32)[:, None, :]
    g1, b1, g2, b2 = row2(ln1_g), row2(ln1_b), row2(ln2_g), row2(ln2_b)

    cos_p, sin_p = _rope_angles(jnp.arange(seq, dtype=jnp.int32))
    cos_pl, sin_pl = _rope_lane_tables(cos_p, sin_p)
    cos_pt, sin_pt = cos_p.T, sin_p.T
    cos_s, sin_s = _rope_lane_tables(*_rope_angles(past_len + jnp.arange(t_new, dtype=jnp.int32)))
    cos_s, sin_s = jnp.tile(cos_s, (bsz, 1)), jnp.tile(sin_s, (bsz, 1))
    tri_p, tri_page = _upper_ones(tm), _upper_ones(PAGE_SIZE)

    page_t = lambda c: jnp.transpose(c, (0, 1, 3, 4, 2)).reshape(depth, n_pool, D_ATT, PAGE_SIZE)
    ckt_moba, cvt_moba = page_t(cache_k_moba), page_t(cache_v_moba)
    ckt_fox, cvt_fox = page_t(cache_k_fox), page_t(cache_v_fox)
    c_lft = jnp.swapaxes(cache_logf_fox.astype(F32), 2, 3)

    hp = x_prompt.reshape(batch * seq, d)
    hs = x_sample.reshape(n_s, d)
    rows_p, rows_s = [], []
    for l in range(depth):
        (qa, vaa, qb, vba, kat, vat, kbt, vbt, kaa, kba, sga, sgb, lft, kmean) = _proj_prompt(
            hp, wn[l], wt[l], wg[l], wft[l], bfp[l], cos_pl, sin_pl, cos_pt, sin_pt, tri_p, batch=batch, tm=tm)
        kmean_h = jnp.transpose(kmean.reshape(batch, seq // tm, N_HEADS, LANES), (0, 2, 1, 3)).astype(BF16)
        ya = _attn_prompt(qa, kaa, vaa, kmean_h, batch=batch, mode="moba")
        yb = _attn_prompt(qb, kba, vba, None, batch=batch, mode="fox")
        h1, comb = _merge(hp, ya, yb, sga, sgb, wbra[l], wbrb[l], wo[l], g1[l], b1[l],
                          wrh[l], wrl[l], br[l], tm=tm, alpha=alpha)
        hp = _moe(h1, comb, wgt[l], wup[l], wdn[l], g2[l], b2[l], tm=tm, alpha=alpha)
        rows_t = lambda a: jnp.transpose(a.reshape(batch, N_HEADS, HEAD_DIM, seq), (0, 3, 1, 2))
        lf_p = jnp.transpose(lft[:N_HEADS].reshape(N_HEADS, batch, seq), (1, 2, 0))
        rows_p.append((rows_t(kat), rows_t(vat), rows_t(kbt), rows_t(vbt), lf_p))

        (qa, ka, va, qb, kb, vb, sga, sgb, lft) = _proj_decode(
            hs, wa[l], wb[l], wg[l], wft[l], bfp[l], cos_s, sin_s)
        pad_new = lambda a: jnp.pad(a.reshape(bsz, t_new, D_ATT), ((0, 0), (0, NEW_ROWS - t_new), (0, 0)))
        rep_q = lambda a: jnp.repeat(a.reshape(bsz, t_new, D_ATT), N_HEADS, axis=1)
        lf_s = lft[:N_HEADS].reshape(N_HEADS, bsz, t_new)
        lf_new = jnp.pad(jnp.transpose(lf_s, (1, 0, 2)), ((0, 0), (0, 0), (0, NEW_ROWS - t_new)))
        ya = _attn_decode(rep_q(qa), pad_new(ka), pad_new(va), None, page_table, ckt_moba, cvt_moba, None,
                          None, layer=l, mode="moba")
        yb = _attn_decode(rep_q(qb), pad_new(kb), pad_new(vb), lf_new, page_table, ckt_fox, cvt_fox, c_lft,
                          tri_page, layer=l, mode="fox")
        h1, comb = _merge(hs, ya.reshape(n_s, D_ATT), yb.reshape(n_s, D_ATT), sga, sgb, wbra[l], wbrb[l],
                          wo[l], g1[l], b1[l], wrh[l], wrl[l], br[l], tm=n_s, alpha=alpha)
        hs = _moe(h1, comb, wgt[l], wup[l], wdn[l], g2[l], b2[l], tm=n_s, alpha=alpha)
        rows_s4 = lambda a: a.reshape(bsz, t_new, N_HEADS, HEAD_DIM)
        rows_s.append((rows_s4(ka), rows_s4(va), rows_s4(kb), rows_s4(vb), jnp.transpose(lf_s, (1, 2, 0))))

    stack = lambda rows, idx: jnp.stack([r[idx] for r in rows], axis=0)
    return (hp.reshape(batch, seq, d), hs.reshape(bsz, t_new, d),
            *[stack(rows_p, i) for i in range(5)],
            *[stack(rows_s, i) for i in range(5)])
```

```python
import functools

import jax
import jax.numpy as jnp
from jax import lax
from jax.experimental import pallas as pl
from jax.experimental.pallas import tpu as pltpu

F32 = jnp.float32
BF16 = jnp.bfloat16

HEAD_DIM = 64
N_HEADS = 8
D_ATT = N_HEADS * HEAD_DIM
MOBA_BLOCK = 256
MOBA_TOPK = 3
PAGE_SIZE = 128
ROPE_THETA = 10000.0
N_GROUPS = 4
EXPERTS_PER_GROUP = 4
N_EXPERTS = N_GROUPS * EXPERTS_PER_GROUP
LN_EPS = 1e-5

LANES = 128
HEAD_ROWS = 16
NEW_ROWS = 16
PAGES_PER_STEP = 32
ATTN_Q_TILE = 1024
ATTN_K_TILE = 1024
MOE_ROWS = 512
MASKED = -1e30
VMEM_LIMIT = 56 * 1024 * 1024


def _dot(a, b):
    return jnp.dot(a, b, preferred_element_type=F32)


def _dot_nt(a, b):
    return lax.dot_general(a, b, (((1,), (1,)), ((), ())), preferred_element_type=F32)


def _split3(x):
    hi = x.astype(BF16)
    r1 = x - hi.astype(F32)
    mid = r1.astype(BF16)
    lo = (r1 - mid.astype(F32)).astype(BF16)
    return hi, mid, lo


def _layer_norm(z, g, b):
    mu = jnp.mean(z, axis=-1, keepdims=True)
    zc = z - mu
    var = jnp.mean(zc * zc, axis=-1, keepdims=True)
    return zc * lax.rsqrt(var + LN_EPS) * g + b


def _log_sigmoid(z):
    return jnp.minimum(z, 0.0) - jnp.log1p(jnp.exp(-jnp.abs(z)))


def _const_spec(shape):
    nd = len(shape)
    return pl.BlockSpec(shape, lambda *_: (0,) * nd, pipeline_mode=pl.Buffered(1))


def _params(*sem):
    return pltpu.CompilerParams(dimension_semantics=sem, vmem_limit_bytes=VMEM_LIMIT)


def _rope_lanes(t, cos, sin):
    lane = lax.broadcasted_iota(jnp.int32, t.shape, 1)
    first_half = (lane & (HEAD_DIM - 1)) < (HEAD_DIM // 2)
    swapped = jnp.where(first_half,
                        pltpu.roll(t, t.shape[1] - HEAD_DIM // 2, 1),
                        pltpu.roll(t, HEAD_DIM // 2, 1))
    return t * cos + swapped * sin


def _rope_rows(t, cos, sin):
    half = HEAD_DIM // 2
    out = []
    for h in range(N_HEADS):
        x1 = t[h * HEAD_DIM:h * HEAD_DIM + half]
        x2 = t[h * HEAD_DIM + half:(h + 1) * HEAD_DIM]
        out += [x1 * cos - x2 * sin, x2 * cos + x1 * sin]
    return jnp.concatenate(out, axis=0)


def _forget_rows(xb, wft_ref, bf_ref):
    return _log_sigmoid(_dot_nt(wft_ref[...], xb) + bf_ref[...])


def _pad_heads(x, fill):
    lane = lax.broadcasted_iota(jnp.int32, (x.shape[0], LANES), 1)
    cols = []
    for h in range(N_HEADS):
        col = x[:, (h // 2) * LANES:(h // 2 + 1) * LANES]
        if h % 2:
            col = pltpu.roll(col, HEAD_DIM, 1)
        cols.append(jnp.where(lane < HEAD_DIM, col, fill(h, lane)))
    return jnp.concatenate(cols, axis=1)


def _aug_rows(kt, extra):
    return jnp.stack([jnp.concatenate([kt[h * HEAD_DIM:(h + 1) * HEAD_DIM], extra(h)], axis=0)
                      for h in range(N_HEADS)], axis=0)


def _ones_lane(h, lane):
    return jnp.where(lane == HEAD_DIM, 1.0, 0.0)


def _fox_q_lanes(h, lane):
    off = lane - HEAD_DIM - h
    return jnp.where((off == 0) | (off == HEAD_ROWS) | (off == 2 * HEAD_ROWS), 1.0, 0.0)


def _proj_prompt_kernel(x_ref, wn_ref, wt_ref, wg_ref, wft_ref, bf_ref, cos_ref, sin_ref, cost_ref, sint_ref,
                        tri_ref, qa_ref, vaa_ref, qb_ref, vba_ref, kat_ref, vat_ref, kbt_ref, vbt_ref,
                        kaa_ref, kba_ref, sga_ref, sgb_ref, lft_ref, kmean_ref, carry_ref,
                        *, tiles_per_seq):
    i = pl.program_id(0)
    tm = x_ref.shape[0]
    xb = x_ref[...].astype(BF16)
    scale = HEAD_DIM ** -0.5
    no_fill = lambda h, lane: 0.0

    pn = _dot(xb, wn_ref[...])
    qa = _rope_lanes(pn[:, :D_ATT], cos_ref[...], sin_ref[...]) * scale
    qa_ref[...] = _pad_heads(qa, no_fill).astype(BF16)
    vaa_ref[...] = _pad_heads(pn[:, D_ATT:2 * D_ATT], _ones_lane).astype(BF16)
    qb_ref[...] = _pad_heads(pn[:, 2 * D_ATT:3 * D_ATT] * scale, _fox_q_lanes).astype(BF16)
    vba_ref[...] = _pad_heads(pn[:, 3 * D_ATT:], _ones_lane).astype(BF16)

    lft = _forget_rows(xb, wft_ref, bf_ref)
    lft_ref[...] = lft

    @pl.when(i % tiles_per_seq == 0)
    def _():
        carry_ref[...] = jnp.zeros_like(carry_ref)

    hi, mid, lo = _split3(lft)
    tri = tri_ref[...]
    ck = _dot(hi, tri) + _dot(mid, tri) + _dot(lo, tri) + carry_ref[...]
    carry_ref[...] = ck[:, tm - 1:tm]

    pt = _dot_nt(wt_ref[...], xb)
    kat = _rope_rows(pt[:D_ATT], cost_ref[...], sint_ref[...])
    kat_ref[...] = kat
    kmean = jnp.mean(kat, axis=1, keepdims=True)
    kmean_ref[...] = _aug_rows(kmean, lambda h: jnp.zeros((LANES - HEAD_DIM, 1), F32))
    blk_row = lax.broadcasted_iota(jnp.int32, (LANES - HEAD_DIM, tm), 0)
    blk_ind = jnp.where(blk_row == i % tiles_per_seq, 1.0, 0.0)
    kaa_ref[...] = _aug_rows(kat, lambda h: blk_ind).astype(BF16)
    vat_ref[...] = pt[D_ATT:2 * D_ATT]
    kbt = pt[2 * D_ATT:3 * D_ATT]
    kbt_ref[...] = kbt
    chi, cmid, clo = _split3(ck)
    bias_rows = -jnp.concatenate([chi.astype(F32), cmid.astype(F32), clo.astype(F32),
                                  jnp.zeros((LANES - HEAD_DIM - 3 * HEAD_ROWS, tm), F32)], axis=0)
    kba_ref[...] = _aug_rows(kbt, lambda h: bias_rows).astype(BF16)
    vbt_ref[...] = pt[3 * D_ATT:]

    pg = _dot(xb, wg_ref[...])
    d_model = pg.shape[1] // 2
    sga_ref[...] = jax.nn.sigmoid(pg[:, :d_model])
    sgb_ref[...] = jax.nn.sigmoid(pg[:, d_model:])


def _proj_prompt(x2d, wn, wt, wg, wft, bfp, cos, sin, cost, sint, tri, *, batch, tm):
    n, d = x2d.shape
    t = n // batch
    tps = t // tm
    nt = n // tm
    aug = N_HEADS * LANES
    row = lambda w: pl.BlockSpec((tm, w), lambda i: (i, 0))
    tab = pl.BlockSpec((tm, D_ATT), lambda i: (i % tps, 0))
    tabt = pl.BlockSpec((HEAD_DIM // 2, tm), lambda i: (0, i % tps))
    heads_t = pl.BlockSpec((HEAD_ROWS, tm), lambda i: (0, i))
    trans = pl.BlockSpec((None, D_ATT, tm), lambda i: (i // tps, 0, i % tps))
    trans_aug = pl.BlockSpec((None, N_HEADS, LANES, tm), lambda i: (i // tps, 0, 0, i % tps))
    out_shape = ([jax.ShapeDtypeStruct((n, aug), BF16)] * 4
                 + [jax.ShapeDtypeStruct((batch, D_ATT, t), F32)] * 4
                 + [jax.ShapeDtypeStruct((batch, N_HEADS, LANES, t), BF16)] * 2
                 + [jax.ShapeDtypeStruct((n, d), F32)] * 2
                 + [jax.ShapeDtypeStruct((HEAD_ROWS, n), F32)]
                 + [jax.ShapeDtypeStruct((nt, N_HEADS, LANES, 1), F32)])
    out_specs = ([row(aug)] * 4 + [trans] * 4 + [trans_aug] * 2 + [row(d)] * 2 + [heads_t]
                 + [pl.BlockSpec((None, N_HEADS, LANES, 1), lambda i: (i, 0, 0, 0))])
    return pl.pallas_call(
        functools.partial(_proj_prompt_kernel, tiles_per_seq=tps),
        grid=(nt,),
        in_specs=[row(d), _const_spec(wn.shape), _const_spec(wt.shape), _const_spec(wg.shape),
                  _const_spec(wft.shape), _const_spec(bfp.shape), tab, tab, tabt, tabt,
                  _const_spec(tri.shape)],
        out_specs=out_specs,
        out_shape=out_shape,
        scratch_shapes=[pltpu.VMEM((HEAD_ROWS, 1), F32)],
        compiler_params=_params("arbitrary"),
        name="proj_prompt",
    )(x2d, wn, wt, wg, wft, bfp, cos, sin, cost, sint, tri)


def _proj_decode_kernel(x_ref, wa_ref, wb_ref, wg_ref, wft_ref, bf_ref, cos_ref, sin_ref,
                        qa_ref, ka_ref, va_ref, qb_ref, kb_ref, vb_ref, sga_ref, sgb_ref, lft_ref):
    xb = x_ref[...].astype(BF16)
    scale = HEAD_DIM ** -0.5
    cos, sin = cos_ref[...], sin_ref[...]
    pa = _dot(xb, wa_ref[...])
    qa_ref[...] = (_rope_lanes(pa[:, :D_ATT], cos, sin) * scale).astype(BF16)
    ka_ref[...] = _rope_lanes(pa[:, D_ATT:2 * D_ATT], cos, sin)
    va_ref[...] = pa[:, 2 * D_ATT:]
    pb = _dot(xb, wb_ref[...])
    qb_ref[...] = (pb[:, :D_ATT] * scale).astype(BF16)
    kb_ref[...] = pb[:, D_ATT:2 * D_ATT]
    vb_ref[...] = pb[:, 2 * D_ATT:]
    pg = _dot(xb, wg_ref[...])
    d_model = pg.shape[1] // 2
    sga_ref[...] = jax.nn.sigmoid(pg[:, :d_model])
    sgb_ref[...] = jax.nn.sigmoid(pg[:, d_model:])
    lft_ref[...] = _forget_rows(xb, wft_ref, bf_ref)


def _proj_decode(x2d, wa, wb, wg, wft, bfp, cos, sin):
    n, d = x2d.shape
    full = lambda a: pl.BlockSpec(a.shape, lambda i: (0,) * a.ndim)
    ins = (x2d, wa, wb, wg, wft, bfp, cos, sin)
    out_shape = ([jax.ShapeDtypeStruct((n, D_ATT), BF16)] + [jax.ShapeDtypeStruct((n, D_ATT), F32)] * 2
                 + [jax.ShapeDtypeStruct((n, D_ATT), BF16)] + [jax.ShapeDtypeStruct((n, D_ATT), F32)] * 2
                 + [jax.ShapeDtypeStruct((n, d), F32)] * 2
                 + [jax.ShapeDtypeStruct((HEAD_ROWS, n), F32)])
    return pl.pallas_call(
        _proj_decode_kernel,
        grid=(1,),
        in_specs=[full(a) for a in ins],
        out_specs=[full(s) for s in out_shape],
        out_shape=out_shape,
        compiler_params=_params("arbitrary"),
        name="proj_decode",
    )(*ins)


def _attn_kernel(q_ref, kt_ref, v_ref, *rest, mode, tk):
    o_ref = rest[-1]
    i = pl.program_id(2)
    tq = q_ref.shape[0]
    n_full = (i * tq) // tk
    off_d = pl.multiple_of(n_full * tk, tk)
    rows = i * tq + lax.broadcasted_iota(jnp.int32, (tq, tk), 0)
    cols = off_d + lax.broadcasted_iota(jnp.int32, (tq, tk), 1)
    causal = cols <= rows

    qs = []
    for hh in range(2):
        q = q_ref[:, hh * LANES:(hh + 1) * LANES]
        if mode == "moba":
            km_ref = rest[0]
            nb = km_ref.shape[1]
            gate = _dot_nt(km_ref[hh], q)
            blk = lax.broadcasted_iota(jnp.int32, (nb, tq), 0)
            own = (i * tq + lax.broadcasted_iota(jnp.int32, (nb, tq), 1)) // MOBA_BLOCK
            g = jnp.where(blk < own, gate, -jnp.inf)
            keep = blk == own
            for _ in range(min(MOBA_TOPK, nb)):
                mx = jnp.max(g, axis=0, keepdims=True)
                first = jnp.min(jnp.where(g == mx, blk, nb), axis=0, keepdims=True)
                pick = (blk == first) & (mx > -jnp.inf)
                keep = keep | pick
                g = jnp.where(pick, -jnp.inf, g)
            bias_t = jnp.concatenate([jnp.zeros((HEAD_DIM, tq), F32), jnp.where(keep, 0.0, MASKED),
                                      jnp.zeros((LANES - HEAD_DIM - nb, tq), F32)], axis=0)
            q = (q.astype(F32) + bias_t.T).astype(BF16)
        qs.append(q)

    def chunk(off, diagonal, state):
        new = []
        for hh in range(2):
            m, acc = state[hh]
            s = _dot(qs[hh], kt_ref[hh, :, pl.ds(off, tk)])
            if diagonal:
                s = jnp.where(causal, s, MASKED)
            m_new = jnp.maximum(m, jnp.max(s, axis=1, keepdims=True))
            p = jnp.exp(s - m_new).astype(BF16)
            acc = jnp.exp(m - m_new) * acc + _dot(p, v_ref[pl.ds(off, tk), hh * LANES:(hh + 1) * LANES])
            new.append((m_new, acc))
        return tuple(new)

    init = tuple((jnp.full((tq, 1), MASKED, F32), jnp.zeros((tq, LANES), F32)) for _ in range(2))
    state = chunk(off_d, True, init)
    state = lax.fori_loop(0, n_full, lambda c, st: chunk(pl.multiple_of(c * tk, tk), False, st), state)

    res = [acc / acc[:, HEAD_DIM:HEAD_DIM + 1] for _, acc in state]
    lane = lax.broadcasted_iota(jnp.int32, (tq, LANES), 1)
    o_ref[...] = jnp.where(lane < HEAD_DIM, res[0], pltpu.roll(res[1], HEAD_DIM, 1)).astype(o_ref.dtype)


def _attn_prompt(q, kt, v, kmean, *, batch, mode):
    n = q.shape[0]
    t = n // batch
    tq, tk = ATTN_Q_TILE, ATTN_K_TILE
    assert t % tk == 0 and tk % tq == 0 and tq % MOBA_BLOCK == 0
    q3, v3 = q.reshape(batch, t, -1), v.reshape(batch, t, -1)
    in_specs = [pl.BlockSpec((None, tq, 2 * LANES), lambda b, hp, i: (b, i, hp)),
                pl.BlockSpec((None, 2, LANES, t), lambda b, hp, i: (b, hp, 0, 0)),
                pl.BlockSpec((None, t, 2 * LANES), lambda b, hp, i: (b, 0, hp))]
    args = [q3, kt, v3]
    if mode == "moba":
        assert kmean.shape[2] <= LANES - HEAD_DIM
        in_specs.append(pl.BlockSpec((None, 2, kmean.shape[2], LANES), lambda b, hp, i: (b, hp, 0, 0)))
        args.append(kmean)
    out = pl.pallas_call(
        functools.partial(_attn_kernel, mode=mode, tk=tk),
        grid=(batch, N_HEADS // 2, t // tq),
        in_specs=in_specs,
        out_specs=pl.BlockSpec((None, tq, LANES), lambda b, hp, i: (b, i, hp)),
        out_shape=jax.ShapeDtypeStruct((batch, t, D_ATT), BF16),
        compiler_params=_params("arbitrary", "arbitrary", "arbitrary"),
        name="attn_" + mode,
    )(*args)
    return out.reshape(n, D_ATT)


def _head_rows_q(q_ref):
    q = q_ref[...]
    r = lax.broadcasted_iota(jnp.int32, q.shape, 0)
    c = lax.broadcasted_iota(jnp.int32, q.shape, 1)
    own = (c // HEAD_DIM) == (r % N_HEADS)
    return jnp.where(own, q, jnp.zeros_like(q)), own


def _head_diag(res, own, t_new):
    kept = jnp.where(own, res, 0.0)
    return jnp.concatenate(
        [jnp.sum(kept[t * N_HEADS:(t + 1) * N_HEADS], axis=0, keepdims=True) for t in range(t_new)], axis=0)


def _sfox_kernel(pt_ref, q_ref, *refs, n_steps, t_new):
    del pt_ref
    npg = PAGES_PER_STEP
    k_refs, v_refs, lf_refs = refs[:npg], refs[npg:2 * npg], refs[2 * npg:3 * npg]
    knew_ref, vnew_ref, lfnew_ref, tri_ref, o_ref, m_scr, l_scr, acc_scr, run_scr = refs[3 * npg:]
    c = pl.program_id(1)
    qbig, own = _head_rows_q(q_ref)

    @pl.when(c == 0)
    def _():
        m_scr[...] = jnp.full_like(m_scr, MASKED)
        l_scr[...] = jnp.zeros_like(l_scr)
        acc_scr[...] = jnp.zeros_like(acc_scr)
        run_scr[...] = jnp.zeros_like(run_scr)

    kt_all = jnp.concatenate([r[...] for r in k_refs], axis=1).astype(BF16)
    vt_all = jnp.concatenate([r[...] for r in v_refs], axis=1).astype(BF16)
    s = _dot(qbig, kt_all)

    lf = jnp.concatenate([r[...] for r in lf_refs], axis=0)
    hi, mid, lo = _split3(lf)
    tri = tri_ref[...]
    local = _dot(hi, tri) + _dot(mid, tri) + _dot(lo, tri)
    run = run_scr[...]
    pieces = []
    for pg in range(npg):
        cp = local[pg * N_HEADS:(pg + 1) * N_HEADS] + run
        pieces.append(cp)
        run = cp[:, PAGE_SIZE - 1:PAGE_SIZE]
    run_scr[...] = run
    ck = jnp.concatenate(pieces, axis=1)
    s = s - jnp.concatenate([ck] * t_new, axis=0)

    m_old = m_scr[...]
    m_new = jnp.maximum(m_old, jnp.max(s, axis=1, keepdims=True))
    a = jnp.exp(m_old - m_new)
    p = jnp.exp(s - m_new)
    l_new = a * l_scr[...] + jnp.sum(p, axis=1, keepdims=True)
    acc_new = a * acc_scr[...] + _dot_nt(p.astype(BF16), vt_all)
    m_scr[...] = m_new
    l_scr[...] = l_new
    acc_scr[...] = acc_new

    @pl.when(c == n_steps - 1)
    def _():
        sn = _dot_nt(qbig, knew_ref[...].astype(BF16))
        lfn = lfnew_ref[...]
        u8 = lax.broadcasted_iota(jnp.int32, lfn.shape, 1)
        cn = run + jnp.zeros_like(lfn)
        for u in range(t_new):
            cn = cn + jnp.where(u8 >= u, lfn[:, u:u + 1], 0.0)
        sn = sn - jnp.concatenate([cn] * t_new, axis=0)
        u = lax.broadcasted_iota(jnp.int32, sn.shape, 1)
        tok = lax.broadcasted_iota(jnp.int32, sn.shape, 0) // N_HEADS
        sn = jnp.where(u <= tok, sn, MASKED)
        m_fin = jnp.maximum(m_new, jnp.max(sn, axis=1, keepdims=True))
        a2 = jnp.exp(m_new - m_fin)
        pn = jnp.exp(sn - m_fin)
        l_fin = a2 * l_new + jnp.sum(pn, axis=1, keepdims=True)
        acc_fin = a2 * acc_new + _dot(pn.astype(BF16), vnew_ref[...].astype(BF16))
        o_ref[...] = _head_diag(acc_fin / l_fin, own, t_new).astype(o_ref.dtype)


def _smoba_kernel(pt_ref, q_ref, *refs, n_steps, t_new):
    del pt_ref
    npg = PAGES_PER_STEP
    k_refs, v_refs = refs[:npg], refs[npg:2 * npg]
    knew_ref, vnew_ref, o_ref, m_scr, l_scr, g_scr, acc_scr = refs[2 * npg:]
    c = pl.program_id(1)
    qbig, own = _head_rows_q(q_ref)
    nrow = qbig.shape[0]
    blocks_per_step = npg * PAGE_SIZE // MOBA_BLOCK
    n_past = n_steps * blocks_per_step
    blk_lane = lax.broadcasted_iota(jnp.int32, (nrow, LANES), 1)

    @pl.when(c == 0)
    def _():
        m_scr[...] = jnp.zeros_like(m_scr)
        l_scr[...] = jnp.zeros_like(l_scr)
        g_scr[...] = jnp.zeros_like(g_scr)

    kt_all = jnp.concatenate([r[...] for r in k_refs], axis=1).astype(BF16)
    vt_all = jnp.concatenate([r[...] for r in v_refs], axis=1).astype(BF16)
    s = _dot(qbig, kt_all)
    m_all, l_all, g_all = m_scr[...], l_scr[...], g_scr[...]
    for jb in range(blocks_per_step):
        sb = s[:, jb * MOBA_BLOCK:(jb + 1) * MOBA_BLOCK]
        gb = jnp.sum(sb, axis=1, keepdims=True)
        mb = jnp.max(sb, axis=1, keepdims=True)
        p = jnp.exp(sb - mb)
        lb = jnp.sum(p, axis=1, keepdims=True)
        blk = c * blocks_per_step + jb
        acc_scr[blk] = _dot_nt(p.astype(BF16), vt_all[:, jb * MOBA_BLOCK:(jb + 1) * MOBA_BLOCK])
        hit = blk_lane == blk
        m_all = jnp.where(hit, mb, m_all)
        l_all = jnp.where(hit, lb, l_all)
        g_all = jnp.where(hit, gb, g_all)
    m_scr[...] = m_all
    l_scr[...] = l_all
    g_scr[...] = g_all

    @pl.when(c == n_steps - 1)
    def _():
        g = jnp.where(blk_lane < n_past, g_all, -jnp.inf)
        sel = blk_lane < 0
        for _ in range(min(MOBA_TOPK, n_past)):
            mx = jnp.max(g, axis=1, keepdims=True)
            first = jnp.min(jnp.where(g == mx, blk_lane, LANES), axis=1, keepdims=True)
            pick = (blk_lane == first) & (mx > -jnp.inf)
            sel = sel | pick
            g = jnp.where(pick, -jnp.inf, g)

        sn = _dot_nt(qbig, knew_ref[...].astype(BF16))
        u = lax.broadcasted_iota(jnp.int32, sn.shape, 1)
        tok = lax.broadcasted_iota(jnp.int32, sn.shape, 0) // N_HEADS
        sn = jnp.where(u <= tok, sn, MASKED)
        m_own = jnp.max(sn, axis=1, keepdims=True)
        p_own = jnp.exp(sn - m_own)
        l_own = jnp.sum(p_own, axis=1, keepdims=True)
        acc_own = _dot(p_own.astype(BF16), vnew_ref[...].astype(BF16))

        m_fin = jnp.maximum(jnp.max(jnp.where(sel, m_all, MASKED), axis=1, keepdims=True), m_own)
        w = jnp.where(sel, jnp.exp(m_all - m_fin), 0.0)
        w_own = jnp.exp(m_own - m_fin)
        l_fin = jnp.sum(jnp.where(sel, w * l_all, 0.0), axis=1, keepdims=True) + w_own * l_own

        def body(j, out):
            wj = jnp.sum(jnp.where(blk_lane == j, w, 0.0), axis=1, keepdims=True)
            return out + wj * acc_scr[j]

        out = lax.fori_loop(0, n_past, body, w_own * acc_own)
        o_ref[...] = _head_diag(out / l_fin, own, t_new).astype(o_ref.dtype)


def _attn_decode(q, k_new, v_new, lf_new, page_table, cache_kt, cache_vt, cache_lft, tri, *, layer, mode):
    bsz, nrow, _ = q.shape
    t_new = nrow // N_HEADS
    n_pages = page_table.shape[1]
    npg = PAGES_PER_STEP
    n_steps = n_pages // npg
    pt_flat = page_table.reshape(-1)

    def page_spec(pg, rows):
        return pl.BlockSpec((None, None, rows, PAGE_SIZE),
                            lambda b, c, pt: (layer, pt[b * n_pages + c * npg + pg], 0, 0))

    per_seq = lambda rows, width: pl.BlockSpec((None, rows, width), lambda b, c, pt: (b, 0, 0))
    kv_specs = [page_spec(pg, D_ATT) for pg in range(npg)]
    in_specs = [per_seq(nrow, D_ATT)] + kv_specs + kv_specs
    args = [q] + [cache_kt] * npg + [cache_vt] * npg
    if mode == "fox":
        in_specs += [page_spec(pg, N_HEADS) for pg in range(npg)]
        args += [cache_lft] * npg
        in_specs += [per_seq(NEW_ROWS, D_ATT), per_seq(NEW_ROWS, D_ATT), per_seq(N_HEADS, NEW_ROWS),
                     pl.BlockSpec(tri.shape, lambda b, c, pt: (0, 0))]
        args += [k_new, v_new, lf_new, tri]
        scratch = [pltpu.VMEM((nrow, 1), F32), pltpu.VMEM((nrow, 1), F32),
                   pltpu.VMEM((nrow, D_ATT), F32), pltpu.VMEM((N_HEADS, 1), F32)]
        body = _sfox_kernel
    else:
        in_specs += [per_seq(NEW_ROWS, D_ATT), per_seq(NEW_ROWS, D_ATT)]
        args += [k_new, v_new]
        n_past = n_steps * npg * PAGE_SIZE // MOBA_BLOCK
        scratch = [pltpu.VMEM((nrow, LANES), F32)] * 3 + [pltpu.VMEM((n_past, nrow, D_ATT), F32)]
        body = _smoba_kernel
    return pl.pallas_call(
        functools.partial(body, n_steps=n_steps, t_new=t_new),
        grid_spec=pltpu.PrefetchScalarGridSpec(
            num_scalar_prefetch=1, grid=(bsz, n_steps), in_specs=in_specs,
            out_specs=pl.BlockSpec((None, t_new, D_ATT), lambda b, c, pt: (b, 0, 0)),
            scratch_shapes=scratch),
        out_shape=jax.ShapeDtypeStruct((bsz, t_new, D_ATT), BF16),
        compiler_params=_params("arbitrary", "arbitrary"),
        name="decode_" + mode,
    )(pt_flat, *args)


def _merge_kernel(x_ref, ya_ref, yb_ref, sga_ref, sgb_ref, wa_ref, wb_ref, wo_ref, g1_ref, b1_ref,
                  wrh_ref, wrl_ref, br_ref, h_ref, comb_ref, *, alpha):
    ua = _dot(ya_ref[...], wa_ref[...])
    ub = _dot(yb_ref[...], wb_ref[...])
    merged = sga_ref[...] * ua + sgb_ref[...] * ub
    mix = _dot(merged.astype(BF16), wo_ref[...])
    h = _layer_norm(alpha * x_ref[...] + mix, g1_ref[...], b1_ref[...])
    h_ref[...] = h

    h_hi = h.astype(BF16)
    h_lo = (h - h_hi.astype(F32)).astype(BF16)
    wrh = wrh_ref[...]
    logit = _dot(h_hi, wrh) + _dot(h_lo, wrh) + _dot(h_hi, wrl_ref[...]) + br_ref[...]
    lane = lax.broadcasted_iota(jnp.int32, logit.shape, 1)
    is_group = (lane >= N_EXPERTS) & (lane < N_EXPERTS + N_GROUPS)
    g_max = jnp.max(jnp.where(is_group, logit, -jnp.inf), axis=1, keepdims=True)
    g_sel = jnp.min(jnp.where(is_group & (logit == g_max), lane - N_EXPERTS, LANES), axis=1, keepdims=True)
    g_sum = jnp.sum(jnp.where(is_group, jnp.exp(logit - g_max), 0.0), axis=1, keepdims=True)
    p_g = 1.0 / g_sum
    in_group = (lane < N_EXPERTS) & ((lane // EXPERTS_PER_GROUP) == g_sel)
    e_max = jnp.max(jnp.where(in_group, logit, -jnp.inf), axis=1, keepdims=True)
    e_exp = jnp.where(in_group, jnp.exp(logit - e_max), 0.0)
    e_prob = e_exp / jnp.sum(e_exp, axis=1, keepdims=True)
    p1 = jnp.max(jnp.where(in_group, e_prob, -1.0), axis=1, keepdims=True)
    i1 = jnp.min(jnp.where(in_group & (e_prob == p1), lane, LANES), axis=1, keepdims=True)
    rest = in_group & (lane != i1)
    p2 = jnp.max(jnp.where(rest, e_prob, -1.0), axis=1, keepdims=True)
    i2 = jnp.min(jnp.where(rest & (e_prob == p2), lane, LANES), axis=1, keepdims=True)
    den = p1 + p2
    comb_ref[...] = p_g * jnp.where(lane == i1, p1 / den, jnp.where(lane == i2, p2 / den, 0.0))


def _merge(x2d, ya, yb, sga, sgb, wa, wb, wo, g1, b1, wrh, wrl, br, *, tm, alpha):
    n, d = x2d.shape
    row = lambda w: pl.BlockSpec((tm, w), lambda i: (i, 0))
    return pl.pallas_call(
        functools.partial(_merge_kernel, alpha=alpha),
        grid=(n // tm,),
        in_specs=[row(d), row(D_ATT), row(D_ATT), row(d), row(d),
                  _const_spec(wa.shape), _const_spec(wb.shape), _const_spec(wo.shape),
                  _const_spec(g1.shape), _const_spec(b1.shape),
                  _const_spec(wrh.shape), _const_spec(wrl.shape), _const_spec(br.shape)],
        out_specs=[row(d), row(LANES)],
        out_shape=[jax.ShapeDtypeStruct((n, d), F32), jax.ShapeDtypeStruct((n, LANES), F32)],
        compiler_params=_params("arbitrary"),
        name="merge",
    )(x2d, ya, yb, sga, sgb, wa, wb, wo, g1, b1, wrh, wrl, br)


def _moe_kernel(h_ref, comb_ref, wg_ref, wu_ref, wd_ref, g2_ref, b2_ref, o_ref, *, alpha):
    h = h_ref[...]
    hb = h.astype(BF16)
    comb = comb_ref[...]
    acc = jnp.zeros_like(h)
    for e in range(N_EXPERTS):
        g = _dot(hb, wg_ref[e])
        u = _dot(hb, wu_ref[e])
        act = g * jax.nn.sigmoid(g) * u * comb[:, e:e + 1]
        acc = acc + _dot(act.astype(BF16), wd_ref[e])
    o_ref[...] = _layer_norm(alpha * h + acc, g2_ref[...], b2_ref[...])


def _moe(h2d, comb, wg, wu, wd, g2, b2, *, tm, alpha):
    n, d = h2d.shape
    row = lambda w: pl.BlockSpec((tm, w), lambda i: (i, 0))
    return pl.pallas_call(
        functools.partial(_moe_kernel, alpha=alpha),
        grid=(n // tm,),
        in_specs=[row(d), row(LANES), _const_spec(wg.shape), _const_spec(wu.shape), _const_spec(wd.shape),
                  _const_spec(g2.shape), _const_spec(b2.shape)],
        out_specs=row(d),
        out_shape=jax.ShapeDtypeStruct((n, d), F32),
        compiler_params=_params("arbitrary"),
        name="moe",
    )(h2d, comb, wg, wu, wd, g2, b2)


def _rope_angles(pos):
    half = HEAD_DIM // 2
    inv_freq = jnp.power(ROPE_THETA, -jnp.arange(half, dtype=F32) / half)
    ang = pos.astype(F32)[:, None] * inv_freq[None, :]
    return jnp.cos(ang), jnp.sin(ang)


def _rope_lane_tables(cos, sin):
    return (jnp.tile(jnp.concatenate([cos, cos], axis=-1), (1, N_HEADS)),
            jnp.tile(jnp.concatenate([-sin, sin], axis=-1), (1, N_HEADS)))


def _upper_ones(n):
    r = lax.broadcasted_iota(jnp.int32, (n, n), 0)
    c = lax.broadcasted_iota(jnp.int32, (n, n), 1)
    return (r <= c).astype(BF16)


def kernel(x_prompt, x_sample, cache_k_moba, cache_v_moba, cache_k_fox, cache_v_fox, cache_logf_fox,
           page_table, w_in, b_f, w_br_a, w_br_b, w_o, ln1_g, ln1_b, w_rg, b_rg, w_re, b_re,
           w_gate, w_up, w_down, ln2_g, ln2_b):
    batch, seq, d = x_prompt.shape
    bsz, t_new, _ = x_sample.shape
    depth = w_in.shape[0]
    n_pool = cache_k_moba.shape[1]
    n_pages = page_table.shape[1]
    past_len = n_pages * PAGE_SIZE
    alpha = (2 * depth) ** 0.25
    tm = MOBA_BLOCK
    n_s = bsz * t_new
    assert seq % tm == 0 and n_pages % PAGES_PER_STEP == 0 and t_new <= NEW_ROWS
    assert past_len // MOBA_BLOCK <= LANES and seq // MOBA_BLOCK <= LANES
    assert w_in.shape[2] == 6 * D_ATT + N_HEADS + 2 * d

    sec = lambda j: w_in[:, :, j * D_ATT:(j + 1) * D_ATT]
    w_in_t = jnp.swapaxes(w_in, 1, 2)
    sec_t = lambda j: w_in_t[:, j * D_ATT:(j + 1) * D_ATT, :]
    wn = jnp.concatenate([sec(0), sec(2), sec(3), sec(5)], axis=2).astype(BF16)
    wt = jnp.concatenate([sec_t(1), sec_t(2), sec_t(4), sec_t(5)], axis=1).astype(BF16)
    wa = w_in[:, :, :3 * D_ATT].astype(BF16)
    wb = w_in[:, :, 3 * D_ATT:6 * D_ATT].astype(BF16)
    wft = jnp.pad(w_in_t[:, 6 * D_ATT:6 * D_ATT + N_HEADS, :],
                  ((0, 0), (0, HEAD_ROWS - N_HEADS), (0, 0))).astype(BF16)
    bfp = jnp.pad(b_f.astype(F32), ((0, 0), (0, HEAD_ROWS - N_HEADS)))[:, :, None]
    wg = w_in[:, :, 6 * D_ATT + N_HEADS:].astype(BF16)
    wbra, wbrb, wo = w_br_a.astype(BF16), w_br_b.astype(BF16), w_o.astype(BF16)
    wr = jnp.pad(jnp.concatenate([w_re, w_rg], axis=-1).astype(F32),
                 ((0, 0), (0, 0), (0, LANES - N_EXPERTS - N_GROUPS)))
    wrh = wr.astype(BF16)
    wrl = (wr - wrh.astype(F32)).astype(BF16)
    br = jnp.pad(jnp.concatenate([b_re, b_rg], axis=-1).astype(F32),
                 ((0, 0), (0, LANES - N_EXPERTS - N_GROUPS)))[:, None, :]
    wgt, wup, wdn = w_gate.astype(BF16), w_up.astype(BF16), w_down.astype(BF16)
    row2 = lambda a: a.astype(F32)[:, None, :]
    g1, b1, g2, b2 = row2(ln1_g), row2(ln1_b), row2(ln2_g), row2(ln2_b)

    cos_p, sin_p = _rope_angles(jnp.arange(seq, dtype=jnp.int32))
    cos_pl, sin_pl = _rope_lane_tables(cos_p, sin_p)
    cos_pt, sin_pt = cos_p.T, sin_p.T
    cos_s, sin_s = _rope_lane_tables(*_rope_angles(past_len + jnp.arange(t_new, dtype=jnp.int32)))
    cos_s, sin_s = jnp.tile(cos_s, (bsz, 1)), jnp.tile(sin_s, (bsz, 1))
    tri_p, tri_page = _upper_ones(tm), _upper_ones(PAGE_SIZE)

    page_t = lambda c: jnp.transpose(c, (0, 1, 3, 4, 2)).reshape(depth, n_pool, D_ATT, PAGE_SIZE)
    ckt_moba, cvt_moba = page_t(cache_k_moba), page_t(cache_v_moba)
    ckt_fox, cvt_fox = page_t(cache_k_fox), page_t(cache_v_fox)
    c_lft = jnp.swapaxes(cache_logf_fox.astype(F32), 2, 3)

    hp = x_prompt.reshape(batch * seq, d)
    hs = x_sample.reshape(n_s, d)
    rows_p, rows_s = [], []
    for l in range(depth):
        (qa, vaa, qb, vba, kat, vat, kbt, vbt, kaa, kba, sga, sgb, lft, kmean) = _proj_prompt(
            hp, wn[l], wt[l], wg[l], wft[l], bfp[l], cos_pl, sin_pl, cos_pt, sin_pt, tri_p, batch=batch, tm=tm)
        kmean_h = jnp.transpose(kmean.reshape(batch, seq // tm, N_HEADS, LANES), (0, 2, 1, 3)).astype(BF16)
        ya = _attn_prompt(qa, kaa, vaa, kmean_h, batch=batch, mode="moba")
        yb = _attn_prompt(qb, kba, vba, None, batch=batch, mode="fox")
        h1, comb = _merge(hp, ya, yb, sga, sgb, wbra[l], wbrb[l], wo[l], g1[l], b1[l],
                          wrh[l], wrl[l], br[l], tm=tm, alpha=alpha)
        hp = _moe(h1, comb, wgt[l], wup[l], wdn[l], g2[l], b2[l], tm=MOE_ROWS, alpha=alpha)
        rows_t = lambda a: jnp.transpose(a.reshape(batch, N_HEADS, HEAD_DIM, seq), (0, 3, 1, 2))
        lf_p = jnp.transpose(lft[:N_HEADS].reshape(N_HEADS, batch, seq), (1, 2, 0))
        rows_p.append((rows_t(kat), rows_t(vat), rows_t(kbt), rows_t(vbt), lf_p))

        (qa, ka, va, qb, kb, vb, sga, sgb, lft) = _proj_decode(
            hs, wa[l], wb[l], wg[l], wft[l], bfp[l], cos_s, sin_s)
        pad_new = lambda a: jnp.pad(a.reshape(bsz, t_new, D_ATT), ((0, 0), (0, NEW_ROWS - t_new), (0, 0)))
        rep_q = lambda a: jnp.repeat(a.reshape(bsz, t_new, D_ATT), N_HEADS, axis=1)
        lf_s = lft[:N_HEADS].reshape(N_HEADS, bsz, t_new)
        lf_new = jnp.pad(jnp.transpose(lf_s, (1, 0, 2)), ((0, 0), (0, 0), (0, NEW_ROWS - t_new)))
        ya = _attn_decode(rep_q(qa), pad_new(ka), pad_new(va), None, page_table, ckt_moba, cvt_moba, None,
                          None, layer=l, mode="moba")
        yb = _attn_decode(rep_q(qb), pad_new(kb), pad_new(vb), lf_new, page_table, ckt_fox, cvt_fox, c_lft,
                          tri_page, layer=l, mode="fox")
        h1, comb = _merge(hs, ya.reshape(n_s, D_ATT), yb.reshape(n_s, D_ATT), sga, sgb, wbra[l], wbrb[l],
                          wo[l], g1[l], b1[l], wrh[l], wrl[l], br[l], tm=n_s, alpha=alpha)
        hs = _moe(h1, comb, wgt[l], wup[l], wdn[l], g2[l], b2[l], tm=n_s, alpha=alpha)
        rows_s4 = lambda a: a.reshape(bsz, t_new, N_HEADS, HEAD_DIM)
        rows_s.append((rows_s4(ka), rows_s4(va), rows_s4(kb), rows_s4(vb), jnp.transpose(lf_s, (1, 2, 0))))

    stack = lambda rows, idx: jnp.stack([r[idx] for r in rows], axis=0)
    return (hp.reshape(batch, seq, d), hs.reshape(bsz, t_new, d),
            *[stack(rows_p, i) for i in range(5)],
            *[stack(rows_s, i) for i in range(5)])
```

```python
import functools

import jax
import jax.numpy as jnp
from jax import lax
from jax.experimental import pallas as pl
from jax.experimental.pallas import tpu as pltpu

F32 = jnp.float32
BF16 = jnp.bfloat16

HEAD_DIM = 64
N_HEADS = 8
D_ATT = N_HEADS * HEAD_DIM
MOBA_BLOCK = 256
MOBA_TOPK = 3
PAGE_SIZE = 128
ROPE_THETA = 10000.0
N_GROUPS = 4
EXPERTS_PER_GROUP = 4
N_EXPERTS = N_GROUPS * EXPERTS_PER_GROUP
LN_EPS = 1e-5

LANES = 128
HEAD_ROWS = 16
NEW_ROWS = 16
PAGES_PER_STEP = 32
ATTN_Q_TILE = 1024
ATTN_K_TILE = 1024
MOE_ROWS = 512
MASKED = -1e30
LOG2_E = 1.4426950408889634
VMEM_LIMIT = 56 * 1024 * 1024


def _dot(a, b):
    return jnp.dot(a, b, preferred_element_type=F32)


def _dot_nt(a, b):
    return lax.dot_general(a, b, (((1,), (1,)), ((), ())), preferred_element_type=F32)


def _split3(x):
    hi = x.astype(BF16)
    r1 = x - hi.astype(F32)
    mid = r1.astype(BF16)
    lo = (r1 - mid.astype(F32)).astype(BF16)
    return hi, mid, lo


def _layer_norm(z, g, b):
    mu = jnp.mean(z, axis=-1, keepdims=True)
    zc = z - mu
    var = jnp.mean(zc * zc, axis=-1, keepdims=True)
    return zc * lax.rsqrt(var + LN_EPS) * g + b


def _log_sigmoid(z):
    return jnp.minimum(z, 0.0) - jnp.log1p(jnp.exp(-jnp.abs(z)))


def _const_spec(shape):
    nd = len(shape)
    return pl.BlockSpec(shape, lambda *_: (0,) * nd, pipeline_mode=pl.Buffered(1))


def _params(*sem):
    return pltpu.CompilerParams(dimension_semantics=sem, vmem_limit_bytes=VMEM_LIMIT)


def _rope_lanes(t, cos, sin):
    lane = lax.broadcasted_iota(jnp.int32, t.shape, 1)
    first_half = (lane & (HEAD_DIM - 1)) < (HEAD_DIM // 2)
    swapped = jnp.where(first_half,
                        pltpu.roll(t, t.shape[1] - HEAD_DIM // 2, 1),
                        pltpu.roll(t, HEAD_DIM // 2, 1))
    return t * cos + swapped * sin


def _rope_rows(t, cos, sin):
    half = HEAD_DIM // 2
    out = []
    for h in range(N_HEADS):
        x1 = t[h * HEAD_DIM:h * HEAD_DIM + half]
        x2 = t[h * HEAD_DIM + half:(h + 1) * HEAD_DIM]
        out += [x1 * cos - x2 * sin, x2 * cos + x1 * sin]
    return jnp.concatenate(out, axis=0)


def _forget_rows(xb, wft_ref, bf_ref):
    return _log_sigmoid(_dot_nt(wft_ref[...], xb) + bf_ref[...])


def _pad_heads(x, fill):
    lane = lax.broadcasted_iota(jnp.int32, (x.shape[0], LANES), 1)
    cols = []
    for h in range(N_HEADS):
        col = x[:, (h // 2) * LANES:(h // 2 + 1) * LANES]
        if h % 2:
            col = pltpu.roll(col, HEAD_DIM, 1)
        cols.append(jnp.where(lane < HEAD_DIM, col, fill(h, lane)))
    return jnp.concatenate(cols, axis=1)


def _aug_rows(kt, extra):
    return jnp.stack([jnp.concatenate([kt[h * HEAD_DIM:(h + 1) * HEAD_DIM], extra(h)], axis=0)
                      for h in range(N_HEADS)], axis=0)


def _ones_lane(h, lane):
    return jnp.where(lane == HEAD_DIM, 1.0, 0.0)


def _fox_q_lanes(h, lane):
    off = lane - HEAD_DIM - h
    return jnp.where((off == 0) | (off == HEAD_ROWS) | (off == 2 * HEAD_ROWS), 1.0, 0.0)


def _proj_prompt_kernel(x_ref, wn_ref, wt_ref, wg_ref, wft_ref, bf_ref, cos_ref, sin_ref, cost_ref, sint_ref,
                        tri_ref, *rest, tiles_per_seq):
    (qa_ref, vaa_ref, qb_ref, vba_ref, kat_ref, vat_ref, kbt_ref, vbt_ref,
     kaa_ref, kba_ref, sga_ref, sgb_ref, lft_ref, kmean_ref, carry_ref) = rest[-15:]
    i = pl.program_id(0)
    tm = x_ref.shape[0]
    xb = x_ref[...].astype(BF16)
    scale = HEAD_DIM ** -0.5 * LOG2_E
    no_fill = lambda h, lane: 0.0

    pn = _dot(xb, wn_ref[...])
    qa = _rope_lanes(pn[:, :D_ATT], cos_ref[...], sin_ref[...]) * scale
    qa_ref[...] = _pad_heads(qa, no_fill).astype(BF16)
    vaa_ref[...] = _pad_heads(pn[:, D_ATT:2 * D_ATT], _ones_lane).astype(BF16)
    qb_ref[...] = _pad_heads(pn[:, 2 * D_ATT:3 * D_ATT] * scale, _fox_q_lanes).astype(BF16)
    vba_ref[...] = _pad_heads(pn[:, 3 * D_ATT:], _ones_lane).astype(BF16)

    lft = _forget_rows(xb, wft_ref, bf_ref)
    lft_ref[...] = lft

    @pl.when(i % tiles_per_seq == 0)
    def _():
        carry_ref[...] = jnp.zeros_like(carry_ref)

    hi, mid, lo = _split3(lft)
    tri = tri_ref[...]
    ck = _dot(hi, tri) + _dot(mid, tri) + _dot(lo, tri) + carry_ref[...]
    carry_ref[...] = ck[:, tm - 1:tm]

    pt = _dot_nt(wt_ref[...], xb)
    kat = _rope_rows(pt[:D_ATT], cost_ref[...], sint_ref[...])
    kat_ref[...] = kat
    kmean = jnp.mean(kat, axis=1, keepdims=True)
    kmean_ref[...] = _aug_rows(kmean, lambda h: jnp.zeros((LANES - HEAD_DIM, 1), F32))
    blk_row = lax.broadcasted_iota(jnp.int32, (LANES - HEAD_DIM, tm), 0)
    blk_ind = jnp.where(blk_row == i % tiles_per_seq, 1.0, 0.0)
    kaa_ref[...] = _aug_rows(kat, lambda h: blk_ind).astype(BF16)
    vat_ref[...] = pt[D_ATT:2 * D_ATT]
    kbt = pt[2 * D_ATT:3 * D_ATT]
    kbt_ref[...] = kbt
    chi, cmid, clo = _split3(ck * LOG2_E)
    bias_rows = -jnp.concatenate([chi.astype(F32), cmid.astype(F32), clo.astype(F32),
                                  jnp.zeros((LANES - HEAD_DIM - 3 * HEAD_ROWS, tm), F32)], axis=0)
    kba_ref[...] = _aug_rows(kbt, lambda h: bias_rows).astype(BF16)
    vbt_ref[...] = pt[3 * D_ATT:]

    pg = _dot(xb, wg_ref[...])
    d_model = pg.shape[1] // 2
    sga_ref[...] = jax.nn.sigmoid(pg[:, :d_model])
    sgb_ref[...] = jax.nn.sigmoid(pg[:, d_model:])


def _proj_prompt(x2d, wn, wt, wg, wft, bfp, cos, sin, cost, sint, tri, kv_all, *, batch, tm, layer, depth):
    n, d = x2d.shape
    t = n // batch
    tps = t // tm
    nt = n // tm
    aug = N_HEADS * LANES
    row = lambda w: pl.BlockSpec((tm, w), lambda i: (i, 0))
    tab = pl.BlockSpec((tm, D_ATT), lambda i: (i % tps, 0))
    tabt = pl.BlockSpec((HEAD_DIM // 2, tm), lambda i: (0, i % tps))
    heads_t = pl.BlockSpec((HEAD_ROWS, tm), lambda i: (0, i))
    trans = pl.BlockSpec((None, None, D_ATT, tm), lambda i: (layer, i // tps, 0, i % tps))
    trans_aug = pl.BlockSpec((None, N_HEADS, LANES, tm), lambda i: (i // tps, 0, 0, i % tps))
    prev = [] if kv_all is None else list(kv_all)
    out_shape = ([jax.ShapeDtypeStruct((n, aug), BF16)] * 4
                 + [jax.ShapeDtypeStruct((depth, batch, D_ATT, t), F32)] * 4
                 + [jax.ShapeDtypeStruct((batch, N_HEADS, LANES, t), BF16)] * 2
                 + [jax.ShapeDtypeStruct((n, d), F32)] * 2
                 + [jax.ShapeDtypeStruct((HEAD_ROWS, n), F32)]
                 + [jax.ShapeDtypeStruct((nt, N_HEADS, LANES, 1), F32)])
    out_specs = ([row(aug)] * 4 + [trans] * 4 + [trans_aug] * 2 + [row(d)] * 2 + [heads_t]
                 + [pl.BlockSpec((None, N_HEADS, LANES, 1), lambda i: (i, 0, 0, 0))])
    return pl.pallas_call(
        functools.partial(_proj_prompt_kernel, tiles_per_seq=tps),
        grid=(nt,),
        in_specs=[row(d), _const_spec(wn.shape), _const_spec(wt.shape), _const_spec(wg.shape),
                  _const_spec(wft.shape), _const_spec(bfp.shape), tab, tab, tabt, tabt,
                  _const_spec(tri.shape)] + [pl.BlockSpec(memory_space=pl.ANY)] * len(prev),
        out_specs=out_specs,
        out_shape=out_shape,
        input_output_aliases={11 + j: 4 + j for j in range(len(prev))},
        scratch_shapes=[pltpu.VMEM((HEAD_ROWS, 1), F32)],
        compiler_params=_params("arbitrary"),
        name="proj_prompt",
    )(x2d, wn, wt, wg, wft, bfp, cos, sin, cost, sint, tri, *prev)


def _proj_decode_kernel(x_ref, wa_ref, wb_ref, wg_ref, wft_ref, bf_ref, cos_ref, sin_ref,
                        qa_ref, ka_ref, va_ref, qb_ref, kb_ref, vb_ref, sga_ref, sgb_ref, lft_ref):
    xb = x_ref[...].astype(BF16)
    scale = HEAD_DIM ** -0.5
    cos, sin = cos_ref[...], sin_ref[...]
    pa = _dot(xb, wa_ref[...])
    qa_ref[...] = (_rope_lanes(pa[:, :D_ATT], cos, sin) * scale).astype(BF16)
    ka_ref[...] = _rope_lanes(pa[:, D_ATT:2 * D_ATT], cos, sin)
    va_ref[...] = pa[:, 2 * D_ATT:]
    pb = _dot(xb, wb_ref[...])
    qb_ref[...] = (pb[:, :D_ATT] * scale).astype(BF16)
    kb_ref[...] = pb[:, D_ATT:2 * D_ATT]
    vb_ref[...] = pb[:, 2 * D_ATT:]
    pg = _dot(xb, wg_ref[...])
    d_model = pg.shape[1] // 2
    sga_ref[...] = jax.nn.sigmoid(pg[:, :d_model])
    sgb_ref[...] = jax.nn.sigmoid(pg[:, d_model:])
    lft_ref[...] = _forget_rows(xb, wft_ref, bf_ref)


def _proj_decode(x2d, wa, wb, wg, wft, bfp, cos, sin):
    n, d = x2d.shape
    full = lambda a: pl.BlockSpec(a.shape, lambda i: (0,) * a.ndim)
    ins = (x2d, wa, wb, wg, wft, bfp, cos, sin)
    out_shape = ([jax.ShapeDtypeStruct((n, D_ATT), BF16)] + [jax.ShapeDtypeStruct((n, D_ATT), F32)] * 2
                 + [jax.ShapeDtypeStruct((n, D_ATT), BF16)] + [jax.ShapeDtypeStruct((n, D_ATT), F32)] * 2
                 + [jax.ShapeDtypeStruct((n, d), F32)] * 2
                 + [jax.ShapeDtypeStruct((HEAD_ROWS, n), F32)])
    return pl.pallas_call(
        _proj_decode_kernel,
        grid=(1,),
        in_specs=[full(a) for a in ins],
        out_specs=[full(s) for s in out_shape],
        out_shape=out_shape,
        compiler_params=_params("arbitrary"),
        name="proj_decode",
    )(*ins)


def _attn_kernel(q_ref, kt_ref, v_ref, *rest, mode, tk):
    o_ref = rest[-1]
    i = pl.program_id(2)
    tq = q_ref.shape[0]
    n_full = i
    off_d = pl.multiple_of(i * tk, tk)

    qs = []
    for hh in range(2):
        q = q_ref[:, hh * LANES:(hh + 1) * LANES]
        if mode == "moba":
            km_ref = rest[0]
            nb = km_ref.shape[1]
            gate = _dot_nt(km_ref[hh], q)
            blk = lax.broadcasted_iota(jnp.int32, (nb, tq), 0)
            own = (i * tq + lax.broadcasted_iota(jnp.int32, (nb, tq), 1)) // MOBA_BLOCK
            g = jnp.where(blk < own, gate, -jnp.inf)
            keep = blk == own
            for _ in range(min(MOBA_TOPK, nb)):
                mx = jnp.max(g, axis=0, keepdims=True)
                first = jnp.min(jnp.where(g == mx, blk, nb), axis=0, keepdims=True)
                pick = (blk == first) & (mx > -jnp.inf)
                keep = keep | pick
                g = jnp.where(pick, -jnp.inf, g)
            bias_t = jnp.concatenate([jnp.zeros((HEAD_DIM, tq), F32), jnp.where(keep, 0.0, MASKED),
                                      jnp.zeros((LANES - HEAD_DIM - nb, tq), F32)], axis=0)
            q = (q.astype(F32) + bias_t.T).astype(BF16)
        qs.append(q)

    def diagonal(r0, ncols):
        half = tq // 2
        rows = r0 + lax.broadcasted_iota(jnp.int32, (half, ncols), 0)
        causal = lax.broadcasted_iota(jnp.int32, (half, ncols), 1) <= rows
        out = []
        for hh in range(2):
            s = _dot(qs[hh][r0:r0 + half], kt_ref[hh, :, pl.ds(off_d, ncols)])
            s = jnp.where(causal, s, MASKED)
            m = jnp.max(s, axis=1, keepdims=True)
            p = jnp.exp2(s - m).astype(BF16)
            out.append((m, _dot(p, v_ref[pl.ds(off_d, ncols), hh * LANES:(hh + 1) * LANES])))
        return out

    def chunk(off, state):
        new = []
        for hh in range(2):
            m, acc = state[hh]
            s = _dot(qs[hh], kt_ref[hh, :, pl.ds(off, tk)])
            m_new = jnp.maximum(m, jnp.max(s, axis=1, keepdims=True))
            p = jnp.exp2(s - m_new).astype(BF16)
            acc = jnp.exp2(m - m_new) * acc + _dot(p, v_ref[pl.ds(off, tk), hh * LANES:(hh + 1) * LANES])
            new.append((m_new, acc))
        return tuple(new)

    top, bottom = diagonal(0, tk // 2), diagonal(tq // 2, tk)
    def rows_max(pair):
        m = jnp.concatenate([pair[0][0], pair[1][0]], axis=0)
        return jnp.max(jnp.broadcast_to(m, (tq, LANES)), axis=1, keepdims=True)

    state = tuple((rows_max((top[hh], bottom[hh])),
                   jnp.concatenate([top[hh][1], bottom[hh][1]], axis=0)) for hh in range(2))
    state = lax.fori_loop(0, n_full, lambda c, st: chunk(pl.multiple_of(c * tk, tk), st), state)

    res = [acc / acc[:, HEAD_DIM:HEAD_DIM + 1] for _, acc in state]
    lane = lax.broadcasted_iota(jnp.int32, (tq, LANES), 1)
    o_ref[...] = jnp.where(lane < HEAD_DIM, res[0], pltpu.roll(res[1], HEAD_DIM, 1)).astype(o_ref.dtype)


def _attn_prompt(q, kt, v, kmean, *, batch, mode):
    n = q.shape[0]
    t = n // batch
    tq, tk = ATTN_Q_TILE, ATTN_K_TILE
    assert t % tk == 0 and tk == tq and tq % (2 * MOBA_BLOCK) == 0
    q3, v3 = q.reshape(batch, t, -1), v.reshape(batch, t, -1)
    in_specs = [pl.BlockSpec((None, tq, 2 * LANES), lambda b, hp, i: (b, i, hp)),
                pl.BlockSpec((None, 2, LANES, t), lambda b, hp, i: (b, hp, 0, 0)),
                pl.BlockSpec((None, t, 2 * LANES), lambda b, hp, i: (b, 0, hp))]
    args = [q3, kt, v3]
    if mode == "moba":
        assert kmean.shape[2] <= LANES - HEAD_DIM
        in_specs.append(pl.BlockSpec((None, 2, kmean.shape[2], LANES), lambda b, hp, i: (b, hp, 0, 0)))
        args.append(kmean)
    out = pl.pallas_call(
        functools.partial(_attn_kernel, mode=mode, tk=tk),
        grid=(batch, N_HEADS // 2, t // tq),
        in_specs=in_specs,
        out_specs=pl.BlockSpec((None, tq, LANES), lambda b, hp, i: (b, i, hp)),
        out_shape=jax.ShapeDtypeStruct((batch, t, D_ATT), BF16),
        compiler_params=_params("arbitrary", "arbitrary", "arbitrary"),
        name="attn_" + mode,
    )(*args)
    return out.reshape(n, D_ATT)


def _head_rows_q(q_ref):
    q = q_ref[...]
    r = lax.broadcasted_iota(jnp.int32, q.shape, 0)
    c = lax.broadcasted_iota(jnp.int32, q.shape, 1)
    own = (c // HEAD_DIM) == (r % N_HEADS)
    return jnp.where(own, q, jnp.zeros_like(q)), own


def _head_diag(res, own, t_new):
    kept = jnp.where(own, res, 0.0)
    return jnp.concatenate(
        [jnp.sum(kept[t * N_HEADS:(t + 1) * N_HEADS], axis=0, keepdims=True) for t in range(t_new)], axis=0)


def _sfox_kernel(pt_ref, q_ref, *refs, n_steps, t_new):
    del pt_ref
    npg = PAGES_PER_STEP
    k_refs, v_refs, lf_refs = refs[:npg], refs[npg:2 * npg], refs[2 * npg:3 * npg]
    knew_ref, vnew_ref, lfnew_ref, tri_ref, o_ref, m_scr, l_scr, acc_scr, run_scr = refs[3 * npg:]
    c = pl.program_id(1)
    qbig, own = _head_rows_q(q_ref)

    @pl.when(c == 0)
    def _():
        m_scr[...] = jnp.full_like(m_scr, MASKED)
        l_scr[...] = jnp.zeros_like(l_scr)
        acc_scr[...] = jnp.zeros_like(acc_scr)
        run_scr[...] = jnp.zeros_like(run_scr)

    kt_all = jnp.concatenate([r[...] for r in k_refs], axis=1).astype(BF16)
    vt_all = jnp.concatenate([r[...] for r in v_refs], axis=1).astype(BF16)
    s = _dot(qbig, kt_all)

    lf = jnp.concatenate([r[...] for r in lf_refs], axis=0)
    hi, mid, lo = _split3(lf)
    tri = tri_ref[...]
    local = _dot(hi, tri) + _dot(mid, tri) + _dot(lo, tri)
    run = run_scr[...]
    pieces = []
    for pg in range(npg):
        cp = local[pg * N_HEADS:(pg + 1) * N_HEADS] + run
        pieces.append(cp)
        run = cp[:, PAGE_SIZE - 1:PAGE_SIZE]
    run_scr[...] = run
    ck = jnp.concatenate(pieces, axis=1)
    s = s - jnp.concatenate([ck] * t_new, axis=0)

    m_old = m_scr[...]
    m_new = jnp.maximum(m_old, jnp.max(s, axis=1, keepdims=True))
    a = jnp.exp(m_old - m_new)
    p = jnp.exp(s - m_new)
    l_new = a * l_scr[...] + jnp.sum(p, axis=1, keepdims=True)
    acc_new = a * acc_scr[...] + _dot_nt(p.astype(BF16), vt_all)
    m_scr[...] = m_new
    l_scr[...] = l_new
    acc_scr[...] = acc_new

    @pl.when(c == n_steps - 1)
    def _():
        sn = _dot_nt(qbig, knew_ref[...].astype(BF16))
        lfn = lfnew_ref[...]
        u8 = lax.broadcasted_iota(jnp.int32, lfn.shape, 1)
        cn = run + jnp.zeros_like(lfn)
        for u in range(t_new):
            cn = cn + jnp.where(u8 >= u, lfn[:, u:u + 1], 0.0)
        sn = sn - jnp.concatenate([cn] * t_new, axis=0)
        u = lax.broadcasted_iota(jnp.int32, sn.shape, 1)
        tok = lax.broadcasted_iota(jnp.int32, sn.shape, 0) // N_HEADS
        sn = jnp.where(u <= tok, sn, MASKED)
        m_fin = jnp.maximum(m_new, jnp.max(sn, axis=1, keepdims=True))
        a2 = jnp.exp(m_new - m_fin)
        pn = jnp.exp(sn - m_fin)
        l_fin = a2 * l_new + jnp.sum(pn, axis=1, keepdims=True)
        acc_fin = a2 * acc_new + _dot(pn.astype(BF16), vnew_ref[...].astype(BF16))
        o_ref[...] = _head_diag(acc_fin / l_fin, own, t_new).astype(o_ref.dtype)


def _smoba_kernel(pt_ref, q_ref, *refs, n_steps, t_new):
    del pt_ref
    npg = PAGES_PER_STEP
    k_refs, v_refs = refs[:npg], refs[npg:2 * npg]
    knew_ref, vnew_ref, o_ref, m_scr, l_scr, g_scr, acc_scr = refs[2 * npg:]
    c = pl.program_id(1)
    qbig, own = _head_rows_q(q_ref)
    nrow = qbig.shape[0]
    blocks_per_step = npg * PAGE_SIZE // MOBA_BLOCK
    n_past = n_steps * blocks_per_step
    blk_lane = lax.broadcasted_iota(jnp.int32, (nrow, LANES), 1)

    @pl.when(c == 0)
    def _():
        m_scr[...] = jnp.zeros_like(m_scr)
        l_scr[...] = jnp.zeros_like(l_scr)
        g_scr[...] = jnp.zeros_like(g_scr)

    kt_all = jnp.concatenate([r[...] for r in k_refs], axis=1).astype(BF16)
    vt_all = jnp.concatenate([r[...] for r in v_refs], axis=1).astype(BF16)
    s = _dot(qbig, kt_all)
    m_all, l_all, g_all = m_scr[...], l_scr[...], g_scr[...]
    for jb in range(blocks_per_step):
        sb = s[:, jb * MOBA_BLOCK:(jb + 1) * MOBA_BLOCK]
        gb = jnp.sum(sb, axis=1, keepdims=True)
        mb = jnp.max(sb, axis=1, keepdims=True)
        p = jnp.exp(sb - mb)
        lb = jnp.sum(p, axis=1, keepdims=True)
        blk = c * blocks_per_step + jb
        acc_scr[blk] = _dot_nt(p.astype(BF16), vt_all[:, jb * MOBA_BLOCK:(jb + 1) * MOBA_BLOCK])
        hit = blk_lane == blk
        m_all = jnp.where(hit, mb, m_all)
        l_all = jnp.where(hit, lb, l_all)
        g_all = jnp.where(hit, gb, g_all)
    m_scr[...] = m_all
    l_scr[...] = l_all
    g_scr[...] = g_all

    @pl.when(c == n_steps - 1)
    def _():
        g = jnp.where(blk_lane < n_past, g_all, -jnp.inf)
        sel = blk_lane < 0
        for _ in range(min(MOBA_TOPK, n_past)):
            mx = jnp.max(g, axis=1, keepdims=True)
            first = jnp.min(jnp.where(g == mx, blk_lane, LANES), axis=1, keepdims=True)
            pick = (blk_lane == first) & (mx > -jnp.inf)
            sel = sel | pick
            g = jnp.where(pick, -jnp.inf, g)

        sn = _dot_nt(qbig, knew_ref[...].astype(BF16))
        u = lax.broadcasted_iota(jnp.int32, sn.shape, 1)
        tok = lax.broadcasted_iota(jnp.int32, sn.shape, 0) // N_HEADS
        sn = jnp.where(u <= tok, sn, MASKED)
        m_own = jnp.max(sn, axis=1, keepdims=True)
        p_own = jnp.exp(sn - m_own)
        l_own = jnp.sum(p_own, axis=1, keepdims=True)
        acc_own = _dot(p_own.astype(BF16), vnew_ref[...].astype(BF16))

        m_fin = jnp.maximum(jnp.max(jnp.where(sel, m_all, MASKED), axis=1, keepdims=True), m_own)
        w = jnp.where(sel, jnp.exp(m_all - m_fin), 0.0)
        w_own = jnp.exp(m_own - m_fin)
        l_fin = jnp.sum(jnp.where(sel, w * l_all, 0.0), axis=1, keepdims=True) + w_own * l_own

        def body(j, out):
            wj = jnp.sum(jnp.where(blk_lane == j, w, 0.0), axis=1, keepdims=True)
            return out + wj * acc_scr[j]

        out = lax.fori_loop(0, n_past, body, w_own * acc_own)
        o_ref[...] = _head_diag(out / l_fin, own, t_new).astype(o_ref.dtype)


def _attn_decode(q, k_new, v_new, lf_new, page_table, cache_kt, cache_vt, cache_lft, tri, *, layer, mode):
    bsz, nrow, _ = q.shape
    t_new = nrow // N_HEADS
    n_pages = page_table.shape[1]
    npg = PAGES_PER_STEP
    n_steps = n_pages // npg
    pt_flat = page_table.reshape(-1)

    def page_spec(pg, rows):
        return pl.BlockSpec((None, None, rows, PAGE_SIZE),
                            lambda b, c, pt: (layer, pt[b * n_pages + c * npg + pg], 0, 0))

    per_seq = lambda rows, width: pl.BlockSpec((None, rows, width), lambda b, c, pt: (b, 0, 0))
    kv_specs = [page_spec(pg, D_ATT) for pg in range(npg)]
    in_specs = [per_seq(nrow, D_ATT)] + kv_specs + kv_specs
    args = [q] + [cache_kt] * npg + [cache_vt] * npg
    if mode == "fox":
        in_specs += [page_spec(pg, N_HEADS) for pg in range(npg)]
        args += [cache_lft] * npg
        in_specs += [per_seq(NEW_ROWS, D_ATT), per_seq(NEW_ROWS, D_ATT), per_seq(N_HEADS, NEW_ROWS),
                     pl.BlockSpec(tri.shape, lambda b, c, pt: (0, 0))]
        args += [k_new, v_new, lf_new, tri]
        scratch = [pltpu.VMEM((nrow, 1), F32), pltpu.VMEM((nrow, 1), F32),
                   pltpu.VMEM((nrow, D_ATT), F32), pltpu.VMEM((N_HEADS, 1), F32)]
        body = _sfox_kernel
    else:
        in_specs += [per_seq(NEW_ROWS, D_ATT), per_seq(NEW_ROWS, D_ATT)]
        args += [k_new, v_new]
        n_past = n_steps * npg * PAGE_SIZE // MOBA_BLOCK
        scratch = [pltpu.VMEM((nrow, LANES), F32)] * 3 + [pltpu.VMEM((n_past, nrow, D_ATT), F32)]
        body = _smoba_kernel
    return pl.pallas_call(
        functools.partial(body, n_steps=n_steps, t_new=t_new),
        grid_spec=pltpu.PrefetchScalarGridSpec(
            num_scalar_prefetch=1, grid=(bsz, n_steps), in_specs=in_specs,
            out_specs=pl.BlockSpec((None, t_new, D_ATT), lambda b, c, pt: (b, 0, 0)),
            scratch_shapes=scratch),
        out_shape=jax.ShapeDtypeStruct((bsz, t_new, D_ATT), BF16),
        compiler_params=_params("arbitrary", "arbitrary"),
        name="decode_" + mode,
    )(pt_flat, *args)


def _merge_kernel(x_ref, ya_ref, yb_ref, sga_ref, sgb_ref, wa_ref, wb_ref, wo_ref, g1_ref, b1_ref,
                  wrh_ref, wrl_ref, br_ref, h_ref, comb_ref, *, alpha):
    ua = _dot(ya_ref[...], wa_ref[...])
    ub = _dot(yb_ref[...], wb_ref[...])
    merged = sga_ref[...] * ua + sgb_ref[...] * ub
    mix = _dot(merged.astype(BF16), wo_ref[...])
    h = _layer_norm(alpha * x_ref[...] + mix, g1_ref[...], b1_ref[...])
    h_ref[...] = h

    h_hi = h.astype(BF16)
    h_lo = (h - h_hi.astype(F32)).astype(BF16)
    wrh = wrh_ref[...]
    logit = _dot(h_hi, wrh) + _dot(h_lo, wrh) + _dot(h_hi, wrl_ref[...]) + br_ref[...]
    lane = lax.broadcasted_iota(jnp.int32, logit.shape, 1)
    is_group = (lane >= N_EXPERTS) & (lane < N_EXPERTS + N_GROUPS)
    g_max = jnp.max(jnp.where(is_group, logit, -jnp.inf), axis=1, keepdims=True)
    g_sel = jnp.min(jnp.where(is_group & (logit == g_max), lane - N_EXPERTS, LANES), axis=1, keepdims=True)
    g_sum = jnp.sum(jnp.where(is_group, jnp.exp(logit - g_max), 0.0), axis=1, keepdims=True)
    p_g = 1.0 / g_sum
    in_group = (lane < N_EXPERTS) & ((lane // EXPERTS_PER_GROUP) == g_sel)
    e_max = jnp.max(jnp.where(in_group, logit, -jnp.inf), axis=1, keepdims=True)
    e_exp = jnp.where(in_group, jnp.exp(logit - e_max), 0.0)
    e_prob = e_exp / jnp.sum(e_exp, axis=1, keepdims=True)
    p1 = jnp.max(jnp.where(in_group, e_prob, -1.0), axis=1, keepdims=True)
    i1 = jnp.min(jnp.where(in_group & (e_prob == p1), lane, LANES), axis=1, keepdims=True)
    rest = in_group & (lane != i1)
    p2 = jnp.max(jnp.where(rest, e_prob, -1.0), axis=1, keepdims=True)
    i2 = jnp.min(jnp.where(rest & (e_prob == p2), lane, LANES), axis=1, keepdims=True)
    den = p1 + p2
    comb_ref[...] = p_g * jnp.where(lane == i1, p1 / den, jnp.where(lane == i2, p2 / den, 0.0))


def _merge(x2d, ya, yb, sga, sgb, wa, wb, wo, g1, b1, wrh, wrl, br, *, tm, alpha):
    n, d = x2d.shape
    row = lambda w: pl.BlockSpec((tm, w), lambda i: (i, 0))
    return pl.pallas_call(
        functools.partial(_merge_kernel, alpha=alpha),
        grid=(n // tm,),
        in_specs=[row(d), row(D_ATT), row(D_ATT), row(d), row(d),
                  _const_spec(wa.shape), _const_spec(wb.shape), _const_spec(wo.shape),
                  _const_spec(g1.shape), _const_spec(b1.shape),
                  _const_spec(wrh.shape), _const_spec(wrl.shape), _const_spec(br.shape)],
        out_specs=[row(d), row(LANES)],
        out_shape=[jax.ShapeDtypeStruct((n, d), F32), jax.ShapeDtypeStruct((n, LANES), F32)],
        compiler_params=_params("arbitrary"),
        name="merge",
    )(x2d, ya, yb, sga, sgb, wa, wb, wo, g1, b1, wrh, wrl, br)


def _moe_kernel(h_ref, comb_ref, wg_ref, wu_ref, wd_ref, g2_ref, b2_ref, o_ref, *, alpha):
    h = h_ref[...]
    hb = h.astype(BF16)
    comb = comb_ref[...]
    acc = jnp.zeros_like(h)
    for e in range(N_EXPERTS):
        g = _dot(hb, wg_ref[e])
        u = _dot(hb, wu_ref[e])
        act = g * jax.nn.sigmoid(g) * u * comb[:, e:e + 1]
        acc = acc + _dot(act.astype(BF16), wd_ref[e])
    o_ref[...] = _layer_norm(alpha * h + acc, g2_ref[...], b2_ref[...])


def _moe(h2d, comb, wg, wu, wd, g2, b2, *, tm, alpha):
    n, d = h2d.shape
    row = lambda w: pl.BlockSpec((tm, w), lambda i: (i, 0))
    return pl.pallas_call(
        functools.partial(_moe_kernel, alpha=alpha),
        grid=(n // tm,),
        in_specs=[row(d), row(LANES), _const_spec(wg.shape), _const_spec(wu.shape), _const_spec(wd.shape),
                  _const_spec(g2.shape), _const_spec(b2.shape)],
        out_specs=row(d),
        out_shape=jax.ShapeDtypeStruct((n, d), F32),
        compiler_params=_params("arbitrary"),
        name="moe",
    )(h2d, comb, wg, wu, wd, g2, b2)


def _rope_angles(pos):
    half = HEAD_DIM // 2
    inv_freq = jnp.power(ROPE_THETA, -jnp.arange(half, dtype=F32) / half)
    ang = pos.astype(F32)[:, None] * inv_freq[None, :]
    return jnp.cos(ang), jnp.sin(ang)


def _rope_lane_tables(cos, sin):
    return (jnp.tile(jnp.concatenate([cos, cos], axis=-1), (1, N_HEADS)),
            jnp.tile(jnp.concatenate([-sin, sin], axis=-1), (1, N_HEADS)))


def _upper_ones(n):
    r = lax.broadcasted_iota(jnp.int32, (n, n), 0)
    c = lax.broadcasted_iota(jnp.int32, (n, n), 1)
    return (r <= c).astype(BF16)


def kernel(x_prompt, x_sample, cache_k_moba, cache_v_moba, cache_k_fox, cache_v_fox, cache_logf_fox,
           page_table, w_in, b_f, w_br_a, w_br_b, w_o, ln1_g, ln1_b, w_rg, b_rg, w_re, b_re,
           w_gate, w_up, w_down, ln2_g, ln2_b):
    batch, seq, d = x_prompt.shape
    bsz, t_new, _ = x_sample.shape
    depth = w_in.shape[0]
    n_pool = cache_k_moba.shape[1]
    n_pages = page_table.shape[1]
    past_len = n_pages * PAGE_SIZE
    alpha = (2 * depth) ** 0.25
    tm = MOBA_BLOCK
    n_s = bsz * t_new
    assert seq % tm == 0 and n_pages % PAGES_PER_STEP == 0 and t_new <= NEW_ROWS
    assert past_len // MOBA_BLOCK <= LANES and seq // MOBA_BLOCK <= LANES
    assert w_in.shape[2] == 6 * D_ATT + N_HEADS + 2 * d

    sec = lambda j: w_in[:, :, j * D_ATT:(j + 1) * D_ATT]
    w_in_t = jnp.swapaxes(w_in, 1, 2)
    sec_t = lambda j: w_in_t[:, j * D_ATT:(j + 1) * D_ATT, :]
    wn = jnp.concatenate([sec(0), sec(2), sec(3), sec(5)], axis=2).astype(BF16)
    wt = jnp.concatenate([sec_t(1), sec_t(2), sec_t(4), sec_t(5)], axis=1).astype(BF16)
    wa = w_in[:, :, :3 * D_ATT].astype(BF16)
    wb = w_in[:, :, 3 * D_ATT:6 * D_ATT].astype(BF16)
    wft = jnp.pad(w_in_t[:, 6 * D_ATT:6 * D_ATT + N_HEADS, :],
                  ((0, 0), (0, HEAD_ROWS - N_HEADS), (0, 0))).astype(BF16)
    bfp = jnp.pad(b_f.astype(F32), ((0, 0), (0, HEAD_ROWS - N_HEADS)))[:, :, None]
    wg = w_in[:, :, 6 * D_ATT + N_HEADS:].astype(BF16)
    wbra, wbrb, wo = w_br_a.astype(BF16), w_br_b.astype(BF16), w_o.astype(BF16)
    wr = jnp.pad(jnp.concatenate([w_re, w_rg], axis=-1).astype(F32),
                 ((0, 0), (0, 0), (0, LANES - N_EXPERTS - N_GROUPS)))
    wrh = wr.astype(BF16)
    wrl = (wr - wrh.astype(F32)).astype(BF16)
    br = jnp.pad(jnp.concatenate([b_re, b_rg], axis=-1).astype(F32),
                 ((0, 0), (0, LANES - N_EXPERTS - N_GROUPS)))[:, None, :]
    wgt, wup, wdn = w_gate.astype(BF16), w_up.astype(BF16), w_down.astype(BF16)
    row2 = lambda a: a.astype(F32)[:, None, :]
    g1, b1, g2, b2 = row2(ln1_g), row2(ln1_b), row2(ln2_g), row2(ln2_b)

    cos_p, sin_p = _rope_angles(jnp.arange(seq, dtype=jnp.int32))
    cos_pl, sin_pl = _rope_lane_tables(cos_p, sin_p)
    cos_pt, sin_pt = cos_p.T, sin_p.T
    cos_s, sin_s = _rope_lane_tables(*_rope_angles(past_len + jnp.arange(t_new, dtype=jnp.int32)))
    cos_s, sin_s = jnp.tile(cos_s, (bsz, 1)), jnp.tile(sin_s, (bsz, 1))
    tri_p, tri_page = _upper_ones(tm), _upper_ones(PAGE_SIZE)

    page_t = lambda c: jnp.transpose(c, (0, 1, 3, 4, 2)).reshape(depth, n_pool, D_ATT, PAGE_SIZE)
    ckt_moba, cvt_moba = page_t(cache_k_moba), page_t(cache_v_moba)
    ckt_fox, cvt_fox = page_t(cache_k_fox), page_t(cache_v_fox)
    c_lft = jnp.swapaxes(cache_logf_fox.astype(F32), 2, 3)

    hp = x_prompt.reshape(batch * seq, d)
    hs = x_sample.reshape(n_s, d)
    lf_p, rows_s, kv_all = [], [], None
    for l in range(depth):
        (qa, vaa, qb, vba, *kv_all, kaa, kba, sga, sgb, lft, kmean) = _proj_prompt(
            hp, wn[l], wt[l], wg[l], wft[l], bfp[l], cos_pl, sin_pl, cos_pt, sin_pt, tri_p, kv_all,
            batch=batch, tm=tm, layer=l, depth=depth)
        kmean_h = jnp.transpose(kmean.reshape(batch, seq // tm, N_HEADS, LANES), (0, 2, 1, 3)).astype(BF16)
        ya = _attn_prompt(qa, kaa, vaa, kmean_h, batch=batch, mode="moba")
        yb = _attn_prompt(qb, kba, vba, None, batch=batch, mode="fox")
        h1, comb = _merge(hp, ya, yb, sga, sgb, wbra[l], wbrb[l], wo[l], g1[l], b1[l],
                          wrh[l], wrl[l], br[l], tm=tm, alpha=alpha)
        hp = _moe(h1, comb, wgt[l], wup[l], wdn[l], g2[l], b2[l], tm=MOE_ROWS, alpha=alpha)
        lf_p.append(jnp.transpose(lft[:N_HEADS].reshape(N_HEADS, batch, seq), (1, 2, 0)))

        (qa, ka, va, qb, kb, vb, sga, sgb, lft) = _proj_decode(
            hs, wa[l], wb[l], wg[l], wft[l], bfp[l], cos_s, sin_s)
        pad_new = lambda a: jnp.pad(a.reshape(bsz, t_new, D_ATT), ((0, 0), (0, NEW_ROWS - t_new), (0, 0)))
        rep_q = lambda a: jnp.repeat(a.reshape(bsz, t_new, D_ATT), N_HEADS, axis=1)
        lf_s = lft[:N_HEADS].reshape(N_HEADS, bsz, t_new)
        lf_new = jnp.pad(jnp.transpose(lf_s, (1, 0, 2)), ((0, 0), (0, 0), (0, NEW_ROWS - t_new)))
        ya = _attn_decode(rep_q(qa), pad_new(ka), pad_new(va), None, page_table, ckt_moba, cvt_moba, None,
                          None, layer=l, mode="moba")
        yb = _attn_decode(rep_q(qb), pad_new(kb), pad_new(vb), lf_new, page_table, ckt_fox, cvt_fox, c_lft,
                          tri_page, layer=l, mode="fox")
        h1, comb = _merge(hs, ya.reshape(n_s, D_ATT), yb.reshape(n_s, D_ATT), sga, sgb, wbra[l], wbrb[l],
                          wo[l], g1[l], b1[l], wrh[l], wrl[l], br[l], tm=n_s, alpha=alpha)
        hs = _moe(h1, comb, wgt[l], wup[l], wdn[l], g2[l], b2[l], tm=n_s, alpha=alpha)
        rows_s4 = lambda a: a.reshape(bsz, t_new, N_HEADS, HEAD_DIM)
        rows_s.append((rows_s4(ka), rows_s4(va), rows_s4(kb), rows_s4(vb), jnp.transpose(lf_s, (1, 2, 0))))

    stack = lambda rows, idx: jnp.stack([r[idx] for r in rows], axis=0)
    rows_t = lambda a: jnp.transpose(a.reshape(depth, batch, N_HEADS, HEAD_DIM, seq), (0, 1, 4, 2, 3))
    return (hp.reshape(batch, seq, d), hs.reshape(bsz, t_new, d),
            *[rows_t(a) for a in kv_all], jnp.stack(lf_p, axis=0),
            *[stack(rows_s, i) for i in range(5)])
```

```python
import functools

import jax
import jax.numpy as jnp
from jax import lax
from jax.experimental import pallas as pl
from jax.experimental.pallas import tpu as pltpu

F32 = jnp.float32
BF16 = jnp.bfloat16

HEAD_DIM = 64
N_HEADS = 8
D_ATT = N_HEADS * HEAD_DIM
MOBA_BLOCK = 256
MOBA_TOPK = 3
PAGE_SIZE = 128
ROPE_THETA = 10000.0
N_GROUPS = 4
EXPERTS_PER_GROUP = 4
N_EXPERTS = N_GROUPS * EXPERTS_PER_GROUP
LN_EPS = 1e-5

LANES = 128
HEAD_ROWS = 16
NEW_ROWS = 16
PAGES_PER_STEP = 32
RING_SLOTS = 3
RING_PAGES = 16
ATTN_Q_TILE = 1024
ATTN_K_TILE = 1024
MOE_ROWS = 512
MASKED = -1e30
LOG2_E = 1.4426950408889634
VMEM_LIMIT = 56 * 1024 * 1024


def _dot(a, b):
    return jnp.dot(a, b, preferred_element_type=F32)


def _dot_nt(a, b):
    return lax.dot_general(a, b, (((1,), (1,)), ((), ())), preferred_element_type=F32)


def _split3(x):
    hi = x.astype(BF16)
    r1 = x - hi.astype(F32)
    mid = r1.astype(BF16)
    lo = (r1 - mid.astype(F32)).astype(BF16)
    return hi, mid, lo


def _layer_norm(z, g, b):
    mu = jnp.mean(z, axis=-1, keepdims=True)
    zc = z - mu
    var = jnp.mean(zc * zc, axis=-1, keepdims=True)
    return zc * lax.rsqrt(var + LN_EPS) * g + b


def _log_sigmoid(z):
    return jnp.minimum(z, 0.0) - jnp.log1p(jnp.exp(-jnp.abs(z)))


def _const_spec(shape):
    nd = len(shape)
    return pl.BlockSpec(shape, lambda *_: (0,) * nd, pipeline_mode=pl.Buffered(1))


def _params(*sem):
    return pltpu.CompilerParams(dimension_semantics=sem, vmem_limit_bytes=VMEM_LIMIT)


def _rope_lanes(t, cos, sin):
    lane = lax.broadcasted_iota(jnp.int32, t.shape, 1)
    first_half = (lane & (HEAD_DIM - 1)) < (HEAD_DIM // 2)
    swapped = jnp.where(first_half,
                        pltpu.roll(t, t.shape[1] - HEAD_DIM // 2, 1),
                        pltpu.roll(t, HEAD_DIM // 2, 1))
    return t * cos + swapped * sin


def _rope_rows(t, cos, sin):
    half = HEAD_DIM // 2
    out = []
    for h in range(N_HEADS):
        x1 = t[h * HEAD_DIM:h * HEAD_DIM + half]
        x2 = t[h * HEAD_DIM + half:(h + 1) * HEAD_DIM]
        out += [x1 * cos - x2 * sin, x2 * cos + x1 * sin]
    return jnp.concatenate(out, axis=0)


def _forget_rows(xb, wft_ref, bf_ref):
    return _log_sigmoid(_dot_nt(wft_ref[...], xb) + bf_ref[...])


def _pad_heads(x, fill):
    lane = lax.broadcasted_iota(jnp.int32, (x.shape[0], LANES), 1)
    cols = []
    for h in range(N_HEADS):
        col = x[:, (h // 2) * LANES:(h // 2 + 1) * LANES]
        if h % 2:
            col = pltpu.roll(col, HEAD_DIM, 1)
        cols.append(jnp.where(lane < HEAD_DIM, col, fill(h, lane)))
    return jnp.concatenate(cols, axis=1)


def _aug_rows(kt, extra):
    return jnp.stack([jnp.concatenate([kt[h * HEAD_DIM:(h + 1) * HEAD_DIM], extra(h)], axis=0)
                      for h in range(N_HEADS)], axis=0)


def _ones_lane(h, lane):
    return jnp.where(lane == HEAD_DIM, 1.0, 0.0)


def _fox_q_lanes(h, lane):
    off = lane - HEAD_DIM - h
    return jnp.where((off == 0) | (off == HEAD_ROWS) | (off == 2 * HEAD_ROWS), 1.0, 0.0)


def _proj_prompt_kernel(x_ref, wn_ref, wt_ref, wg_ref, wft_ref, bf_ref, cos_ref, sin_ref, cost_ref, sint_ref,
                        tri_ref, *rest, tiles_per_seq):
    (qa_ref, vaa_ref, qb_ref, vba_ref, kat_ref, vat_ref, kbt_ref, vbt_ref,
     kaa_ref, kba_ref, sga_ref, sgb_ref, lft_ref, kmean_ref, carry_ref) = rest[-15:]
    i = pl.program_id(0)
    tm = x_ref.shape[0]
    xb = x_ref[...].astype(BF16)
    scale = HEAD_DIM ** -0.5 * LOG2_E
    no_fill = lambda h, lane: 0.0

    pn = _dot(xb, wn_ref[...])
    qa = _rope_lanes(pn[:, :D_ATT], cos_ref[...], sin_ref[...]) * scale
    qa_ref[...] = _pad_heads(qa, no_fill).astype(BF16)
    vaa_ref[...] = _pad_heads(pn[:, D_ATT:2 * D_ATT], _ones_lane).astype(BF16)
    qb_ref[...] = _pad_heads(pn[:, 2 * D_ATT:3 * D_ATT] * scale, _fox_q_lanes).astype(BF16)
    vba_ref[...] = _pad_heads(pn[:, 3 * D_ATT:], _ones_lane).astype(BF16)

    lft = _forget_rows(xb, wft_ref, bf_ref)
    lft_ref[...] = lft

    @pl.when(i % tiles_per_seq == 0)
    def _():
        carry_ref[...] = jnp.zeros_like(carry_ref)

    hi, mid, lo = _split3(lft)
    tri = tri_ref[...]
    ck = _dot(hi, tri) + _dot(mid, tri) + _dot(lo, tri) + carry_ref[...]
    carry_ref[...] = ck[:, tm - 1:tm]

    pt = _dot_nt(wt_ref[...], xb)
    kat = _rope_rows(pt[:D_ATT], cost_ref[...], sint_ref[...])
    kat_ref[...] = kat
    kmean = jnp.mean(kat, axis=1, keepdims=True)
    kmean_ref[...] = _aug_rows(kmean, lambda h: jnp.zeros((LANES - HEAD_DIM, 1), F32))
    blk_row = lax.broadcasted_iota(jnp.int32, (LANES - HEAD_DIM, tm), 0)
    blk_ind = jnp.where(blk_row == i % tiles_per_seq, 1.0, 0.0)
    kaa_ref[...] = _aug_rows(kat, lambda h: blk_ind).astype(BF16)
    vat_ref[...] = pt[D_ATT:2 * D_ATT]
    kbt = pt[2 * D_ATT:3 * D_ATT]
    kbt_ref[...] = kbt
    chi, cmid, clo = _split3(ck * LOG2_E)
    bias_rows = -jnp.concatenate([chi.astype(F32), cmid.astype(F32), clo.astype(F32),
                                  jnp.zeros((LANES - HEAD_DIM - 3 * HEAD_ROWS, tm), F32)], axis=0)
    kba_ref[...] = _aug_rows(kbt, lambda h: bias_rows).astype(BF16)
    vbt_ref[...] = pt[3 * D_ATT:]

    pg = _dot(xb, wg_ref[...])
    d_model = pg.shape[1] // 2
    sga_ref[...] = jax.nn.sigmoid(pg[:, :d_model])
    sgb_ref[...] = jax.nn.sigmoid(pg[:, d_model:])


def _proj_prompt(x2d, wn, wt, wg, wft, bfp, cos, sin, cost, sint, tri, kv_all, *, batch, tm, layer, depth):
    n, d = x2d.shape
    t = n // batch
    tps = t // tm
    nt = n // tm
    aug = N_HEADS * LANES
    row = lambda w: pl.BlockSpec((tm, w), lambda i: (i, 0))
    tab = pl.BlockSpec((tm, D_ATT), lambda i: (i % tps, 0))
    tabt = pl.BlockSpec((HEAD_DIM // 2, tm), lambda i: (0, i % tps))
    heads_t = pl.BlockSpec((HEAD_ROWS, tm), lambda i: (0, i))
    trans = pl.BlockSpec((None, None, D_ATT, tm), lambda i: (layer, i // tps, 0, i % tps))
    trans_aug = pl.BlockSpec((None, N_HEADS, LANES, tm), lambda i: (i // tps, 0, 0, i % tps))
    prev = [] if kv_all is None else list(kv_all)
    out_shape = ([jax.ShapeDtypeStruct((n, aug), BF16)] * 4
                 + [jax.ShapeDtypeStruct((depth, batch, D_ATT, t), F32)] * 4
                 + [jax.ShapeDtypeStruct((batch, N_HEADS, LANES, t), BF16)] * 2
                 + [jax.ShapeDtypeStruct((n, d), F32)] * 2
                 + [jax.ShapeDtypeStruct((HEAD_ROWS, n), F32)]
                 + [jax.ShapeDtypeStruct((nt, N_HEADS, LANES, 1), F32)])
    out_specs = ([row(aug)] * 4 + [trans] * 4 + [trans_aug] * 2 + [row(d)] * 2 + [heads_t]
                 + [pl.BlockSpec((None, N_HEADS, LANES, 1), lambda i: (i, 0, 0, 0))])
    return pl.pallas_call(
        functools.partial(_proj_prompt_kernel, tiles_per_seq=tps),
        grid=(nt,),
        in_specs=[row(d), _const_spec(wn.shape), _const_spec(wt.shape), _const_spec(wg.shape),
                  _const_spec(wft.shape), _const_spec(bfp.shape), tab, tab, tabt, tabt,
                  _const_spec(tri.shape)] + [pl.BlockSpec(memory_space=pl.ANY)] * len(prev),
        out_specs=out_specs,
        out_shape=out_shape,
        input_output_aliases={11 + j: 4 + j for j in range(len(prev))},
        scratch_shapes=[pltpu.VMEM((HEAD_ROWS, 1), F32)],
        compiler_params=_params("arbitrary"),
        name="proj_prompt",
    )(x2d, wn, wt, wg, wft, bfp, cos, sin, cost, sint, tri, *prev)


def _proj_decode_kernel(x_ref, wa_ref, wb_ref, wg_ref, wft_ref, bf_ref, cos_ref, sin_ref,
                        qa_ref, ka_ref, va_ref, qb_ref, kb_ref, vb_ref, sga_ref, sgb_ref, lft_ref):
    xb = x_ref[...].astype(BF16)
    scale = HEAD_DIM ** -0.5
    cos, sin = cos_ref[...], sin_ref[...]
    pa = _dot(xb, wa_ref[...])
    qa_ref[...] = (_rope_lanes(pa[:, :D_ATT], cos, sin) * scale).astype(BF16)
    ka_ref[...] = _rope_lanes(pa[:, D_ATT:2 * D_ATT], cos, sin)
    va_ref[...] = pa[:, 2 * D_ATT:]
    pb = _dot(xb, wb_ref[...])
    qb_ref[...] = (pb[:, :D_ATT] * scale).astype(BF16)
    kb_ref[...] = pb[:, D_ATT:2 * D_ATT]
    vb_ref[...] = pb[:, 2 * D_ATT:]
    pg = _dot(xb, wg_ref[...])
    d_model = pg.shape[1] // 2
    sga_ref[...] = jax.nn.sigmoid(pg[:, :d_model])
    sgb_ref[...] = jax.nn.sigmoid(pg[:, d_model:])
    lft_ref[...] = _forget_rows(xb, wft_ref, bf_ref)


def _proj_decode(x2d, wa, wb, wg, wft, bfp, cos, sin):
    n, d = x2d.shape
    full = lambda a: pl.BlockSpec(a.shape, lambda i: (0,) * a.ndim)
    ins = (x2d, wa, wb, wg, wft, bfp, cos, sin)
    out_shape = ([jax.ShapeDtypeStruct((n, D_ATT), BF16)] + [jax.ShapeDtypeStruct((n, D_ATT), F32)] * 2
                 + [jax.ShapeDtypeStruct((n, D_ATT), BF16)] + [jax.ShapeDtypeStruct((n, D_ATT), F32)] * 2
                 + [jax.ShapeDtypeStruct((n, d), F32)] * 2
                 + [jax.ShapeDtypeStruct((HEAD_ROWS, n), F32)])
    return pl.pallas_call(
        _proj_decode_kernel,
        grid=(1,),
        in_specs=[full(a) for a in ins],
        out_specs=[full(s) for s in out_shape],
        out_shape=out_shape,
        compiler_params=_params("arbitrary"),
        name="proj_decode",
    )(*ins)


def _attn_kernel(q_ref, kt_ref, v_ref, *rest, mode, tk):
    o_ref = rest[-1]
    i = pl.program_id(2)
    tq = q_ref.shape[0]
    n_full = i
    off_d = pl.multiple_of(i * tk, tk)

    qs = []
    for hh in range(2):
        q = q_ref[:, hh * LANES:(hh + 1) * LANES]
        if mode == "moba":
            km_ref = rest[0]
            nb = km_ref.shape[1]
            gate = _dot_nt(km_ref[hh], q)
            blk = lax.broadcasted_iota(jnp.int32, (nb, tq), 0)
            own = (i * tq + lax.broadcasted_iota(jnp.int32, (nb, tq), 1)) // MOBA_BLOCK
            g = jnp.where(blk < own, gate, -jnp.inf)
            keep = blk == own
            for _ in range(min(MOBA_TOPK, nb)):
                mx = jnp.max(g, axis=0, keepdims=True)
                first = jnp.min(jnp.where(g == mx, blk, nb), axis=0, keepdims=True)
                pick = (blk == first) & (mx > -jnp.inf)
                keep = keep | pick
                g = jnp.where(pick, -jnp.inf, g)
            bias_t = jnp.concatenate([jnp.zeros((HEAD_DIM, tq), F32), jnp.where(keep, 0.0, MASKED),
                                      jnp.zeros((LANES - HEAD_DIM - nb, tq), F32)], axis=0)
            q = (q.astype(F32) + bias_t.T).astype(BF16)
        qs.append(q)

    def diagonal(r0, ncols):
        half = tq // 2
        rows = r0 + lax.broadcasted_iota(jnp.int32, (half, ncols), 0)
        causal = lax.broadcasted_iota(jnp.int32, (half, ncols), 1) <= rows
        out = []
        for hh in range(2):
            s = _dot(qs[hh][r0:r0 + half], kt_ref[hh, :, pl.ds(off_d, ncols)])
            s = jnp.where(causal, s, MASKED)
            m = jnp.max(s, axis=1, keepdims=True)
            p = jnp.exp2(s - m).astype(BF16)
            out.append((m, _dot(p, v_ref[pl.ds(off_d, ncols), hh * LANES:(hh + 1) * LANES])))
        return out

    def chunk(off, state):
        new = []
        for hh in range(2):
            m, acc = state[hh]
            s = _dot(qs[hh], kt_ref[hh, :, pl.ds(off, tk)])
            m_new = jnp.maximum(m, jnp.max(s, axis=1, keepdims=True))
            p = jnp.exp2(s - m_new).astype(BF16)
            acc = jnp.exp2(m - m_new) * acc + _dot(p, v_ref[pl.ds(off, tk), hh * LANES:(hh + 1) * LANES])
            new.append((m_new, acc))
        return tuple(new)

    top, bottom = diagonal(0, tk // 2), diagonal(tq // 2, tk)
    def rows_max(pair):
        m = jnp.concatenate([pair[0][0], pair[1][0]], axis=0)
        return jnp.max(jnp.broadcast_to(m, (tq, LANES)), axis=1, keepdims=True)

    state = tuple((rows_max((top[hh], bottom[hh])),
                   jnp.concatenate([top[hh][1], bottom[hh][1]], axis=0)) for hh in range(2))
    state = lax.fori_loop(0, n_full, lambda c, st: chunk(pl.multiple_of(c * tk, tk), st), state)

    res = [acc / acc[:, HEAD_DIM:HEAD_DIM + 1] for _, acc in state]
    lane = lax.broadcasted_iota(jnp.int32, (tq, LANES), 1)
    o_ref[...] = jnp.where(lane < HEAD_DIM, res[0], pltpu.roll(res[1], HEAD_DIM, 1)).astype(o_ref.dtype)


def _attn_prompt(q, kt, v, kmean, *, batch, mode):
    n = q.shape[0]
    t = n // batch
    tq, tk = ATTN_Q_TILE, ATTN_K_TILE
    assert t % tk == 0 and tk == tq and tq % (2 * MOBA_BLOCK) == 0
    q3, v3 = q.reshape(batch, t, -1), v.reshape(batch, t, -1)
    in_specs = [pl.BlockSpec((None, tq, 2 * LANES), lambda b, hp, i: (b, i, hp)),
                pl.BlockSpec((None, 2, LANES, t), lambda b, hp, i: (b, hp, 0, 0)),
                pl.BlockSpec((None, t, 2 * LANES), lambda b, hp, i: (b, 0, hp))]
    args = [q3, kt, v3]
    if mode == "moba":
        assert kmean.shape[2] <= LANES - HEAD_DIM
        in_specs.append(pl.BlockSpec((None, 2, kmean.shape[2], LANES), lambda b, hp, i: (b, hp, 0, 0)))
        args.append(kmean)
    out = pl.pallas_call(
        functools.partial(_attn_kernel, mode=mode, tk=tk),
        grid=(batch, N_HEADS // 2, t // tq),
        in_specs=in_specs,
        out_specs=pl.BlockSpec((None, tq, LANES), lambda b, hp, i: (b, i, hp)),
        out_shape=jax.ShapeDtypeStruct((batch, t, D_ATT), BF16),
        compiler_params=_params("arbitrary", "arbitrary", "arbitrary"),
        name="attn_" + mode,
    )(*args)
    return out.reshape(n, D_ATT)


def _head_rows_q(q_ref):
    q = q_ref[...]
    r = lax.broadcasted_iota(jnp.int32, q.shape, 0)
    c = lax.broadcasted_iota(jnp.int32, q.shape, 1)
    own = (c // HEAD_DIM) == (r % N_HEADS)
    return jnp.where(own, q, jnp.zeros_like(q)), own


def _head_diag(res, own, t_new):
    kept = jnp.where(own, res, 0.0)
    return jnp.concatenate(
        [jnp.sum(kept[t * N_HEADS:(t + 1) * N_HEADS], axis=0, keepdims=True) for t in range(t_new)], axis=0)


def _sfox_kernel(pt_ref, q_ref, *refs, n_steps, t_new):
    del pt_ref
    npg = PAGES_PER_STEP
    k_refs, v_refs, lf_refs = refs[:npg], refs[npg:2 * npg], refs[2 * npg:3 * npg]
    knew_ref, vnew_ref, lfnew_ref, tri_ref, o_ref, m_scr, l_scr, acc_scr, run_scr = refs[3 * npg:]
    c = pl.program_id(1)
    qbig, own = _head_rows_q(q_ref)

    @pl.when(c == 0)
    def _():
        m_scr[...] = jnp.full_like(m_scr, MASKED)
        l_scr[...] = jnp.zeros_like(l_scr)
        acc_scr[...] = jnp.zeros_like(acc_scr)
        run_scr[...] = jnp.zeros_like(run_scr)

    kt_all = jnp.concatenate([r[...] for r in k_refs], axis=1).astype(BF16)
    vt_all = jnp.concatenate([r[...] for r in v_refs], axis=1).astype(BF16)
    s = _dot(qbig, kt_all)

    lf = jnp.concatenate([r[...] for r in lf_refs], axis=0)
    hi, mid, lo = _split3(lf)
    tri = tri_ref[...]
    local = _dot(hi, tri) + _dot(mid, tri) + _dot(lo, tri)
    run = run_scr[...]
    pieces = []
    for pg in range(npg):
        cp = local[pg * N_HEADS:(pg + 1) * N_HEADS] + run
        pieces.append(cp)
        run = cp[:, PAGE_SIZE - 1:PAGE_SIZE]
    run_scr[...] = run
    ck = jnp.concatenate(pieces, axis=1)
    s = s - jnp.concatenate([ck] * t_new, axis=0)

    m_old = m_scr[...]
    m_new = jnp.maximum(m_old, jnp.max(s, axis=1, keepdims=True))
    a = jnp.exp(m_old - m_new)
    p = jnp.exp(s - m_new)
    l_new = a * l_scr[...] + jnp.sum(p, axis=1, keepdims=True)
    acc_new = a * acc_scr[...] + _dot_nt(p.astype(BF16), vt_all)
    m_scr[...] = m_new
    l_scr[...] = l_new
    acc_scr[...] = acc_new

    @pl.when(c == n_steps - 1)
    def _():
        sn = _dot_nt(qbig, knew_ref[...].astype(BF16))
        lfn = lfnew_ref[...]
        u8 = lax.broadcasted_iota(jnp.int32, lfn.shape, 1)
        cn = run + jnp.zeros_like(lfn)
        for u in range(t_new):
            cn = cn + jnp.where(u8 >= u, lfn[:, u:u + 1], 0.0)
        sn = sn - jnp.concatenate([cn] * t_new, axis=0)
        u = lax.broadcasted_iota(jnp.int32, sn.shape, 1)
        tok = lax.broadcasted_iota(jnp.int32, sn.shape, 0) // N_HEADS
        sn = jnp.where(u <= tok, sn, MASKED)
        m_fin = jnp.maximum(m_new, jnp.max(sn, axis=1, keepdims=True))
        a2 = jnp.exp(m_new - m_fin)
        pn = jnp.exp(sn - m_fin)
        l_fin = a2 * l_new + jnp.sum(pn, axis=1, keepdims=True)
        acc_fin = a2 * acc_new + _dot(pn.astype(BF16), vnew_ref[...].astype(BF16))
        o_ref[...] = _head_diag(acc_fin / l_fin, own, t_new).astype(o_ref.dtype)


def _smoba_kernel(pt_ref, q_ref, k_hbm, v_hbm, knew_ref, vnew_ref, o_ref, m_scr, l_scr, g_scr, acc_scr,
                  kbuf, vbuf, sem, *, n_steps, t_new, layer):
    npg = RING_PAGES
    c = pl.program_id(1)
    step = pl.program_id(0) * n_steps + c
    total = pl.num_programs(0) * n_steps

    def copies(t, slot):
        out = []
        for pg in range(npg):
            page = pt_ref[t * npg + pg]
            out.append(pltpu.make_async_copy(k_hbm.at[layer, page], kbuf.at[slot, pg], sem.at[slot, 0]))
            out.append(pltpu.make_async_copy(v_hbm.at[layer, page], vbuf.at[slot, pg], sem.at[slot, 1]))
        return out

    @pl.when(step == 0)
    def _():
        for t in range(RING_SLOTS - 1):
            for cp in copies(t, t):
                cp.start()

    ahead = step + RING_SLOTS - 1

    @pl.when(ahead < total)
    def _():
        for cp in copies(ahead, ahead % RING_SLOTS):
            cp.start()

    slot = step % RING_SLOTS
    for cp in copies(step, slot):
        cp.wait()
    k_refs = [kbuf.at[slot, pg] for pg in range(npg)]
    v_refs = [vbuf.at[slot, pg] for pg in range(npg)]
    qbig, own = _head_rows_q(q_ref)
    nrow = qbig.shape[0]
    blocks_per_step = npg * PAGE_SIZE // MOBA_BLOCK
    n_past = n_steps * blocks_per_step
    blk_lane = lax.broadcasted_iota(jnp.int32, (nrow, LANES), 1)

    @pl.when(c == 0)
    def _():
        m_scr[...] = jnp.zeros_like(m_scr)
        l_scr[...] = jnp.zeros_like(l_scr)
        g_scr[...] = jnp.zeros_like(g_scr)

    kt_all = jnp.concatenate([r[...] for r in k_refs], axis=1).astype(BF16)
    vt_all = jnp.concatenate([r[...] for r in v_refs], axis=1).astype(BF16)
    s = _dot(qbig, kt_all)
    m_all, l_all, g_all = m_scr[...], l_scr[...], g_scr[...]
    for jb in range(blocks_per_step):
        sb = s[:, jb * MOBA_BLOCK:(jb + 1) * MOBA_BLOCK]
        gb = jnp.sum(sb, axis=1, keepdims=True)
        mb = jnp.max(sb, axis=1, keepdims=True)
        p = jnp.exp(sb - mb)
        lb = jnp.sum(p, axis=1, keepdims=True)
        blk = c * blocks_per_step + jb
        acc_scr[blk] = _dot_nt(p.astype(BF16), vt_all[:, jb * MOBA_BLOCK:(jb + 1) * MOBA_BLOCK])
        hit = blk_lane == blk
        m_all = jnp.where(hit, mb, m_all)
        l_all = jnp.where(hit, lb, l_all)
        g_all = jnp.where(hit, gb, g_all)
    m_scr[...] = m_all
    l_scr[...] = l_all
    g_scr[...] = g_all

    @pl.when(c == n_steps - 1)
    def _():
        g = jnp.where(blk_lane < n_past, g_all, -jnp.inf)
        sel = blk_lane < 0
        for _ in range(min(MOBA_TOPK, n_past)):
            mx = jnp.max(g, axis=1, keepdims=True)
            first = jnp.min(jnp.where(g == mx, blk_lane, LANES), axis=1, keepdims=True)
            pick = (blk_lane == first) & (mx > -jnp.inf)
            sel = sel | pick
            g = jnp.where(pick, -jnp.inf, g)

        sn = _dot_nt(qbig, knew_ref[...].astype(BF16))
        u = lax.broadcasted_iota(jnp.int32, sn.shape, 1)
        tok = lax.broadcasted_iota(jnp.int32, sn.shape, 0) // N_HEADS
        sn = jnp.where(u <= tok, sn, MASKED)
        m_own = jnp.max(sn, axis=1, keepdims=True)
        p_own = jnp.exp(sn - m_own)
        l_own = jnp.sum(p_own, axis=1, keepdims=True)
        acc_own = _dot(p_own.astype(BF16), vnew_ref[...].astype(BF16))

        m_fin = jnp.maximum(jnp.max(jnp.where(sel, m_all, MASKED), axis=1, keepdims=True), m_own)
        w = jnp.where(sel, jnp.exp(m_all - m_fin), 0.0)
        w_own = jnp.exp(m_own - m_fin)
        l_fin = jnp.sum(jnp.where(sel, w * l_all, 0.0), axis=1, keepdims=True) + w_own * l_own

        def body(j, out):
            wj = jnp.sum(jnp.where(blk_lane == j, w, 0.0), axis=1, keepdims=True)
            return out + wj * acc_scr[j]

        out = lax.fori_loop(0, n_past, body, w_own * acc_own)
        o_ref[...] = _head_diag(out / l_fin, own, t_new).astype(o_ref.dtype)


def _attn_decode(q, k_new, v_new, lf_new, page_table, cache_kt, cache_vt, cache_lft, tri, *, layer, mode):
    bsz, nrow, _ = q.shape
    t_new = nrow // N_HEADS
    n_pages = page_table.shape[1]
    npg = PAGES_PER_STEP
    n_steps = n_pages // npg
    pt_flat = page_table.reshape(-1)

    def page_spec(pg, rows):
        return pl.BlockSpec((None, None, rows, PAGE_SIZE),
                            lambda b, c, pt: (layer, pt[b * n_pages + c * npg + pg], 0, 0))

    per_seq = lambda rows, width: pl.BlockSpec((None, rows, width), lambda b, c, pt: (b, 0, 0))
    kv_specs = [page_spec(pg, D_ATT) for pg in range(npg)]
    in_specs = [per_seq(nrow, D_ATT)] + kv_specs + kv_specs
    args = [q] + [cache_kt] * npg + [cache_vt] * npg
    if mode == "fox":
        in_specs += [page_spec(pg, N_HEADS) for pg in range(npg)]
        args += [cache_lft] * npg
        in_specs += [per_seq(NEW_ROWS, D_ATT), per_seq(NEW_ROWS, D_ATT), per_seq(N_HEADS, NEW_ROWS),
                     pl.BlockSpec(tri.shape, lambda b, c, pt: (0, 0))]
        args += [k_new, v_new, lf_new, tri]
        scratch = [pltpu.VMEM((nrow, 1), F32), pltpu.VMEM((nrow, 1), F32),
                   pltpu.VMEM((nrow, D_ATT), F32), pltpu.VMEM((N_HEADS, 1), F32)]
        body = _sfox_kernel
    else:
        npg = RING_PAGES
        n_steps = n_pages // npg
        in_specs = [per_seq(nrow, D_ATT), pl.BlockSpec(memory_space=pl.ANY), pl.BlockSpec(memory_space=pl.ANY),
                    per_seq(NEW_ROWS, D_ATT), per_seq(NEW_ROWS, D_ATT)]
        args = [q, cache_kt, cache_vt, k_new, v_new]
        n_past = n_steps * npg * PAGE_SIZE // MOBA_BLOCK
        scratch = ([pltpu.VMEM((nrow, LANES), F32)] * 3 + [pltpu.VMEM((n_past, nrow, D_ATT), F32)]
                   + [pltpu.VMEM((RING_SLOTS, npg, D_ATT, PAGE_SIZE), F32)] * 2
                   + [pltpu.SemaphoreType.DMA((RING_SLOTS, 2))])
        body = functools.partial(_smoba_kernel, layer=layer)
    return pl.pallas_call(
        functools.partial(body, n_steps=n_steps, t_new=t_new),
        grid_spec=pltpu.PrefetchScalarGridSpec(
            num_scalar_prefetch=1, grid=(bsz, n_steps), in_specs=in_specs,
            out_specs=pl.BlockSpec((None, t_new, D_ATT), lambda b, c, pt: (b, 0, 0)),
            scratch_shapes=scratch),
        out_shape=jax.ShapeDtypeStruct((bsz, t_new, D_ATT), BF16),
        compiler_params=_params("arbitrary", "arbitrary"),
        name="decode_" + mode,
    )(pt_flat, *args)


def _merge_kernel(x_ref, ya_ref, yb_ref, sga_ref, sgb_ref, wa_ref, wb_ref, wo_ref, g1_ref, b1_ref,
                  wrh_ref, wrl_ref, br_ref, h_ref, comb_ref, *, alpha):
    ua = _dot(ya_ref[...], wa_ref[...])
    ub = _dot(yb_ref[...], wb_ref[...])
    merged = sga_ref[...] * ua + sgb_ref[...] * ub
    mix = _dot(merged.astype(BF16), wo_ref[...])
    h = _layer_norm(alpha * x_ref[...] + mix, g1_ref[...], b1_ref[...])
    h_ref[...] = h

    h_hi = h.astype(BF16)
    h_lo = (h - h_hi.astype(F32)).astype(BF16)
    wrh = wrh_ref[...]
    logit = _dot(h_hi, wrh) + _dot(h_lo, wrh) + _dot(h_hi, wrl_ref[...]) + br_ref[...]
    lane = lax.broadcasted_iota(jnp.int32, logit.shape, 1)
    is_group = (lane >= N_EXPERTS) & (lane < N_EXPERTS + N_GROUPS)
    g_max = jnp.max(jnp.where(is_group, logit, -jnp.inf), axis=1, keepdims=True)
    g_sel = jnp.min(jnp.where(is_group & (logit == g_max), lane - N_EXPERTS, LANES), axis=1, keepdims=True)
    g_sum = jnp.sum(jnp.where(is_group, jnp.exp(logit - g_max), 0.0), axis=1, keepdims=True)
    p_g = 1.0 / g_sum
    in_group = (lane < N_EXPERTS) & ((lane // EXPERTS_PER_GROUP) == g_sel)
    e_max = jnp.max(jnp.where(in_group, logit, -jnp.inf), axis=1, keepdims=True)
    e_exp = jnp.where(in_group, jnp.exp(logit - e_max), 0.0)
    e_prob = e_exp / jnp.sum(e_exp, axis=1, keepdims=True)
    p1 = jnp.max(jnp.where(in_group, e_prob, -1.0), axis=1, keepdims=True)
    i1 = jnp.min(jnp.where(in_group & (e_prob == p1), lane, LANES), axis=1, keepdims=True)
    rest = in_group & (lane != i1)
    p2 = jnp.max(jnp.where(rest, e_prob, -1.0), axis=1, keepdims=True)
    i2 = jnp.min(jnp.where(rest & (e_prob == p2), lane, LANES), axis=1, keepdims=True)
    den = p1 + p2
    comb_ref[...] = p_g * jnp.where(lane == i1, p1 / den, jnp.where(lane == i2, p2 / den, 0.0))


def _merge(x2d, ya, yb, sga, sgb, wa, wb, wo, g1, b1, wrh, wrl, br, *, tm, alpha):
    n, d = x2d.shape
    row = lambda w: pl.BlockSpec((tm, w), lambda i: (i, 0))
    return pl.pallas_call(
        functools.partial(_merge_kernel, alpha=alpha),
        grid=(n // tm,),
        in_specs=[row(d), row(D_ATT), row(D_ATT), row(d), row(d),
                  _const_spec(wa.shape), _const_spec(wb.shape), _const_spec(wo.shape),
                  _const_spec(g1.shape), _const_spec(b1.shape),
                  _const_spec(wrh.shape), _const_spec(wrl.shape), _const_spec(br.shape)],
        out_specs=[row(d), row(LANES)],
        out_shape=[jax.ShapeDtypeStruct((n, d), F32), jax.ShapeDtypeStruct((n, LANES), F32)],
        compiler_params=_params("arbitrary"),
        name="merge",
    )(x2d, ya, yb, sga, sgb, wa, wb, wo, g1, b1, wrh, wrl, br)


def _moe_kernel(h_ref, comb_ref, wg_ref, wu_ref, wd_ref, g2_ref, b2_ref, o_ref, *, alpha):
    h = h_ref[...]
    hb = h.astype(BF16)
    comb = comb_ref[...]
    acc = jnp.zeros_like(h)
    for e in range(N_EXPERTS):
        g = _dot(hb, wg_ref[e])
        u = _dot(hb, wu_ref[e])
        act = g * jax.nn.sigmoid(g) * u * comb[:, e:e + 1]
        acc = acc + _dot(act.astype(BF16), wd_ref[e])
    o_ref[...] = _layer_norm(alpha * h + acc, g2_ref[...], b2_ref[...])


def _moe(h2d, comb, wg, wu, wd, g2, b2, *, tm, alpha):
    n, d = h2d.shape
    row = lambda w: pl.BlockSpec((tm, w), lambda i: (i, 0))
    return pl.pallas_call(
        functools.partial(_moe_kernel, alpha=alpha),
        grid=(n // tm,),
        in_specs=[row(d), row(LANES), _const_spec(wg.shape), _const_spec(wu.shape), _const_spec(wd.shape),
                  _const_spec(g2.shape), _const_spec(b2.shape)],
        out_specs=row(d),
        out_shape=jax.ShapeDtypeStruct((n, d), F32),
        compiler_params=_params("arbitrary"),
        name="moe",
    )(h2d, comb, wg, wu, wd, g2, b2)


def _rope_angles(pos):
    half = HEAD_DIM // 2
    inv_freq = jnp.power(ROPE_THETA, -jnp.arange(half, dtype=F32) / half)
    ang = pos.astype(F32)[:, None] * inv_freq[None, :]
    return jnp.cos(ang), jnp.sin(ang)


def _rope_lane_tables(cos, sin):
    return (jnp.tile(jnp.concatenate([cos, cos], axis=-1), (1, N_HEADS)),
            jnp.tile(jnp.concatenate([-sin, sin], axis=-1), (1, N_HEADS)))


def _upper_ones(n):
    r = lax.broadcasted_iota(jnp.int32, (n, n), 0)
    c = lax.broadcasted_iota(jnp.int32, (n, n), 1)
    return (r <= c).astype(BF16)


def kernel(x_prompt, x_sample, cache_k_moba, cache_v_moba, cache_k_fox, cache_v_fox, cache_logf_fox,
           page_table, w_in, b_f, w_br_a, w_br_b, w_o, ln1_g, ln1_b, w_rg, b_rg, w_re, b_re,
           w_gate, w_up, w_down, ln2_g, ln2_b):
    batch, seq, d = x_prompt.shape
    bsz, t_new, _ = x_sample.shape
    depth = w_in.shape[0]
    n_pool = cache_k_moba.shape[1]
    n_pages = page_table.shape[1]
    past_len = n_pages * PAGE_SIZE
    alpha = (2 * depth) ** 0.25
    tm = MOBA_BLOCK
    n_s = bsz * t_new
    assert seq % tm == 0 and n_pages % PAGES_PER_STEP == 0 and t_new <= NEW_ROWS
    assert past_len // MOBA_BLOCK <= LANES and seq // MOBA_BLOCK <= LANES
    assert w_in.shape[2] == 6 * D_ATT + N_HEADS + 2 * d

    sec = lambda j: w_in[:, :, j * D_ATT:(j + 1) * D_ATT]
    w_in_t = jnp.swapaxes(w_in, 1, 2)
    sec_t = lambda j: w_in_t[:, j * D_ATT:(j + 1) * D_ATT, :]
    wn = jnp.concatenate([sec(0), sec(2), sec(3), sec(5)], axis=2).astype(BF16)
    wt = jnp.concatenate([sec_t(1), sec_t(2), sec_t(4), sec_t(5)], axis=1).astype(BF16)
    wa = w_in[:, :, :3 * D_ATT].astype(BF16)
    wb = w_in[:, :, 3 * D_ATT:6 * D_ATT].astype(BF16)
    wft = jnp.pad(w_in_t[:, 6 * D_ATT:6 * D_ATT + N_HEADS, :],
                  ((0, 0), (0, HEAD_ROWS - N_HEADS), (0, 0))).astype(BF16)
    bfp = jnp.pad(b_f.astype(F32), ((0, 0), (0, HEAD_ROWS - N_HEADS)))[:, :, None]
    wg = w_in[:, :, 6 * D_ATT + N_HEADS:].astype(BF16)
    wbra, wbrb, wo = w_br_a.astype(BF16), w_br_b.astype(BF16), w_o.astype(BF16)
    wr = jnp.pad(jnp.concatenate([w_re, w_rg], axis=-1).astype(F32),
                 ((0, 0), (0, 0), (0, LANES - N_EXPERTS - N_GROUPS)))
    wrh = wr.astype(BF16)
    wrl = (wr - wrh.astype(F32)).astype(BF16)
    br = jnp.pad(jnp.concatenate([b_re, b_rg], axis=-1).astype(F32),
                 ((0, 0), (0, LANES - N_EXPERTS - N_GROUPS)))[:, None, :]
    wgt, wup, wdn = w_gate.astype(BF16), w_up.astype(BF16), w_down.astype(BF16)
    row2 = lambda a: a.astype(F32)[:, None, :]
    g1, b1, g2, b2 = row2(ln1_g), row2(ln1_b), row2(ln2_g), row2(ln2_b)

    cos_p, sin_p = _rope_angles(jnp.arange(seq, dtype=jnp.int32))
    cos_pl, sin_pl = _rope_lane_tables(cos_p, sin_p)
    cos_pt, sin_pt = cos_p.T, sin_p.T
    cos_s, sin_s = _rope_lane_tables(*_rope_angles(past_len + jnp.arange(t_new, dtype=jnp.int32)))
    cos_s, sin_s = jnp.tile(cos_s, (bsz, 1)), jnp.tile(sin_s, (bsz, 1))
    tri_p, tri_page = _upper_ones(tm), _upper_ones(PAGE_SIZE)

    page_t = lambda c: jnp.transpose(c, (0, 1, 3, 4, 2)).reshape(depth, n_pool, D_ATT, PAGE_SIZE)
    ckt_moba, cvt_moba = page_t(cache_k_moba), page_t(cache_v_moba)
    ckt_fox, cvt_fox = page_t(cache_k_fox), page_t(cache_v_fox)
    c_lft = jnp.swapaxes(cache_logf_fox.astype(F32), 2, 3)

    hp = x_prompt.reshape(batch * seq, d)
    hs = x_sample.reshape(n_s, d)
    lf_p, rows_s, kv_all = [], [], None
    for l in range(depth):
        (qa, vaa, qb, vba, *kv_all, kaa, kba, sga, sgb, lft, kmean) = _proj_prompt(
            hp, wn[l], wt[l], wg[l], wft[l], bfp[l], cos_pl, sin_pl, cos_pt, sin_pt, tri_p, kv_all,
            batch=batch, tm=tm, layer=l, depth=depth)
        kmean_h = jnp.transpose(kmean.reshape(batch, seq // tm, N_HEADS, LANES), (0, 2, 1, 3)).astype(BF16)
        ya = _attn_prompt(qa, kaa, vaa, kmean_h, batch=batch, mode="moba")
        yb = _attn_prompt(qb, kba, vba, None, batch=batch, mode="fox")
        h1, comb = _merge(hp, ya, yb, sga, sgb, wbra[l], wbrb[l], wo[l], g1[l], b1[l],
                          wrh[l], wrl[l], br[l], tm=tm, alpha=alpha)
        hp = _moe(h1, comb, wgt[l], wup[l], wdn[l], g2[l], b2[l], tm=MOE_ROWS, alpha=alpha)
        lf_p.append(jnp.transpose(lft[:N_HEADS].reshape(N_HEADS, batch, seq), (1, 2, 0)))

        (qa, ka, va, qb, kb, vb, sga, sgb, lft) = _proj_decode(
            hs, wa[l], wb[l], wg[l], wft[l], bfp[l], cos_s, sin_s)
        pad_new = lambda a: jnp.pad(a.reshape(bsz, t_new, D_ATT), ((0, 0), (0, NEW_ROWS - t_new), (0, 0)))
        rep_q = lambda a: jnp.repeat(a.reshape(bsz, t_new, D_ATT), N_HEADS, axis=1)
        lf_s = lft[:N_HEADS].reshape(N_HEADS, bsz, t_new)
        lf_new = jnp.pad(jnp.transpose(lf_s, (1, 0, 2)), ((0, 0), (0, 0), (0, NEW_ROWS - t_new)))
        ya = _attn_decode(rep_q(qa), pad_new(ka), pad_new(va), None, page_table, ckt_moba, cvt_moba, None,
                          None, layer=l, mode="moba")
        yb = _attn_decode(rep_q(qb), pad_new(kb), pad_new(vb), lf_new, page_table, ckt_fox, cvt_fox, c_lft,
                          tri_page, layer=l, mode="fox")
        h1, comb = _merge(hs, ya.reshape(n_s, D_ATT), yb.reshape(n_s, D_ATT), sga, sgb, wbra[l], wbrb[l],
                          wo[l], g1[l], b1[l], wrh[l], wrl[l], br[l], tm=n_s, alpha=alpha)
        hs = _moe(h1, comb, wgt[l], wup[l], wdn[l], g2[l], b2[l], tm=n_s, alpha=alpha)
        rows_s4 = lambda a: a.reshape(bsz, t_new, N_HEADS, HEAD_DIM)
        rows_s.append((rows_s4(ka), rows_s4(va), rows_s4(kb), rows_s4(vb), jnp.transpose(lf_s, (1, 2, 0))))

    stack = lambda rows, idx: jnp.stack([r[idx] for r in rows], axis=0)
    rows_t = lambda a: jnp.transpose(a.reshape(depth, batch, N_HEADS, HEAD_DIM, seq), (0, 1, 4, 2, 3))
    return (hp.reshape(batch, seq, d), hs.reshape(bsz, t_new, d),
            *[rows_t(a) for a in kv_all], jnp.stack(lf_p, axis=0),
            *[stack(rows_s, i) for i in range(5)])
```
